```python
import math
import jax, jax.numpy as jnp
from jax import lax
import numpy as np

D_MODEL = 2048
BATCH = 4
SEQ = 4096
DEPTH = 1
DEC_BATCH = 32
DEC_SEQ = 1
PAST_LEN = 16384
PAGE_SIZE = 128

HEAD_DIM = 128
H_FOX = 8
H_DIFF = 4
DIFF_DIM = 2 * HEAD_DIM
FOX_W = H_FOX * HEAD_DIM
DIFF_W = H_DIFF * DIFF_DIM
MIX_W = FOX_W + DIFF_W
IN_W = 3 * FOX_W + H_FOX + 3 * DIFF_W
N_EXPERTS = 32
TOP_K = 4
D_FF = D_MODEL
SWIGLU_ALPHA = 1.702
SWIGLU_LIMIT = 7.0
PLE_DIM = 256
Q_BLOCK = 128
LN_EPS = 1e-5
RMS_EPS = 1e-5
DEEP_ALPHA = (2 * DEPTH) ** 0.25
DEEP_BETA = (8 * DEPTH) ** -0.25
FORGET_BIAS_INIT = 3.0
NEG_INF = -1e30

kernel_name = 'hybrid_fox_diffattn_moe_decode_step'


def _layer_norm(x, g, b):
    xf = x.astype(jnp.float32)
    mu = jnp.mean(xf, -1, keepdims=True)
    var = jnp.mean(jnp.square(xf - mu), -1, keepdims=True)
    return ((xf - mu) * lax.rsqrt(var + LN_EPS) * g + b).astype(x.dtype)


def _rms_norm(x, g):
    xf = x.astype(jnp.float32)
    return (xf * lax.rsqrt(jnp.mean(jnp.square(xf), -1, keepdims=True) + RMS_EPS) * g).astype(x.dtype)


def _project(x, w_in, b_fgate):
    proj = x @ w_in
    lead = x.shape[:-1]
    o1, o2, o3, o4 = FOX_W, 2 * FOX_W, 3 * FOX_W, 3 * FOX_W + H_FOX
    q_f = proj[..., :o1].reshape(lead + (H_FOX, HEAD_DIM))
    k_f = proj[..., o1:o2].reshape(lead + (H_FOX, HEAD_DIM))
    v_f = proj[..., o2:o3].reshape(lead + (H_FOX, HEAD_DIM))
    logf = jax.nn.log_sigmoid((proj[..., o3:o4] + b_fgate).astype(jnp.float32))
    q_d = proj[..., o4:o4 + DIFF_W].reshape(lead + (H_DIFF, DIFF_DIM))
    k_d = proj[..., o4 + DIFF_W:o4 + 2 * DIFF_W].reshape(lead + (H_DIFF, DIFF_DIM))
    v_d = proj[..., o4 + 2 * DIFF_W:].reshape(lead + (H_DIFF, DIFF_DIM))
    return q_f, k_f, v_f, logf, q_d, k_d, v_d


def _fox_core(q, k, v, c_q, c_k, t_q, t_k):
    s = jnp.einsum('bqhd,bkhd->bhqk', q, k).astype(jnp.float32) * (HEAD_DIM ** -0.5)
    decay = jnp.swapaxes(c_q, 1, 2)[..., :, None] - jnp.swapaxes(c_k, 1, 2)[..., None, :]
    causal = t_k[None, :] <= t_q[:, None]
    p = jax.nn.softmax(jnp.where(causal, s + decay, NEG_INF), axis=-1)
    return jnp.einsum('bhqk,bkhd->bqhd', p.astype(v.dtype), v)


def _diff_core(q, k, v, lam, t_q, t_k):
    slopes = 2.0 ** (-8.0 * jnp.arange(1, H_DIFF + 1, dtype=jnp.float32) / H_DIFF)
    alibi = -slopes[:, None, None] * jnp.abs((t_q[:, None] - t_k[None, :]).astype(jnp.float32))
    causal = t_k[None, :] <= t_q[:, None]
    scale = HEAD_DIM ** -0.5

    def attn_map(qh, kh):
        s = jnp.einsum('bqhd,bkhd->bhqk', qh, kh).astype(jnp.float32) * scale + alibi
        return jax.nn.softmax(jnp.where(causal, s, NEG_INF), axis=-1)

    a = attn_map(q[..., :HEAD_DIM], k[..., :HEAD_DIM]) - lam * attn_map(q[..., HEAD_DIM:], k[..., HEAD_DIM:])
    return jnp.einsum('bhqk,bkhd->bqhd', a.astype(v.dtype), v)


def _diff_lambda(lq1, lk1, lq2, lk2, lam_init):
    e1 = jnp.exp(jnp.sum(lq1.astype(jnp.float32) * lk1.astype(jnp.float32)))
    e2 = jnp.exp(jnp.sum(lq2.astype(jnp.float32) * lk2.astype(jnp.float32)))
    return e1 - e2 + lam_init


def _prompt_attention(q_f, k_f, v_f, logf, q_d, k_d, v_d, lam):
    b, s = q_f.shape[:2]
    nb = s // Q_BLOCK
    t = jnp.arange(s, dtype=jnp.int32)
    c = jnp.cumsum(logf, axis=1)

    def blocks(a):
        return jnp.swapaxes(a.reshape((b, nb, Q_BLOCK) + a.shape[2:]), 0, 1)

    def one_block(args):
        qf, cq, qd, tq = args
        return _fox_core(qf, k_f, v_f, cq, c, tq, t), _diff_core(qd, k_d, v_d, lam, tq, t)

    of, od = lax.map(one_block, (blocks(q_f), blocks(c), blocks(q_d), t.reshape(nb, Q_BLOCK)))

    def unblock(a):
        return jnp.swapaxes(a, 0, 1).reshape((b, s) + a.shape[3:])

    return unblock(of), unblock(od)


def _sample_attention(layer, q_f, k_f, v_f, logf, q_d, k_d, v_d, lam,
                      cache_fox_k, cache_fox_v, cache_fox_logf, cache_diff_k, cache_diff_v, page_table):
    n_new = q_f.shape[1]
    past = page_table.shape[1] * PAGE_SIZE
    t_q = past + jnp.arange(n_new, dtype=jnp.int32)
    t_k = jnp.arange(past + n_new, dtype=jnp.int32)

    def gather(cache, pt):
        rows = cache[layer, pt]
        return rows.reshape((past,) + rows.shape[2:])

    def one_seq(args):
        qf, kf, vf, lf, qd, kd, vd, pt = args
        kf_all = jnp.concatenate([gather(cache_fox_k, pt), kf], 0)[None]
        vf_all = jnp.concatenate([gather(cache_fox_v, pt), vf], 0)[None]
        lf_all = jnp.concatenate([gather(cache_fox_logf, pt).astype(jnp.float32), lf], 0)
        c = jnp.cumsum(lf_all, axis=0)[None]
        kd_all = jnp.concatenate([gather(cache_diff_k, pt), kd], 0)[None]
        vd_all = jnp.concatenate([gather(cache_diff_v, pt), vd], 0)[None]
        of = _fox_core(qf[None], kf_all, vf_all, c[:, past:], c, t_q, t_k)
        od = _diff_core(qd[None], kd_all, vd_all, lam, t_q, t_k)
        return of[0], od[0]

    return lax.map(one_seq, (q_f, k_f, v_f, logf, q_d, k_d, v_d, page_table))


def _mix_out(o_f, o_d, g_subln, lam_init, w_o):
    od = _rms_norm(o_d, g_subln) * (1.0 - lam_init)
    o = jnp.concatenate([o_f.reshape(o_f.shape[:-2] + (FOX_W,)), od.reshape(od.shape[:-2] + (DIFF_W,))], -1)
    return o @ w_o


def _moe(h, w_router, b_router, w_mlp1, b_mlp1, w_mlp2, b_mlp2):
    lead = h.shape[:-1]
    t = h.reshape(-1, D_MODEL)
    logits = (t @ w_router).astype(jnp.float32) + b_router
    vals, idx = lax.top_k(logits, TOP_K)
    wts = jax.nn.softmax(vals, axis=-1)
    gate = jnp.sum(jax.nn.one_hot(idx, N_EXPERTS, dtype=jnp.float32) * wts[..., None], axis=-2)
    out = jnp.zeros(t.shape, jnp.float32)
    for e in range(N_EXPERTS):
        u = t @ w_mlp1[e] + b_mlp1[e]
        glu = jnp.minimum(u[:, ::2], SWIGLU_LIMIT)
        lin = jnp.clip(u[:, 1::2], -SWIGLU_LIMIT, SWIGLU_LIMIT)
        a = glu * jax.nn.sigmoid(SWIGLU_ALPHA * glu) * (lin + 1.0)
        out = out + gate[:, e:e + 1] * (a @ w_mlp2[e] + b_mlp2[e])
    return out.astype(h.dtype).reshape(lead + (D_MODEL,))


def _finish_layer(x, mix, p, ln1_g, ln1_b, w_router, b_router, w_mlp1, b_mlp1, w_mlp2, b_mlp2,
                  w_pe, w_pg, ln2_g, ln2_b):
    h = _layer_norm(DEEP_ALPHA * x + mix, ln1_g, ln1_b)
    ple = jax.nn.sigmoid(h @ w_pg) * (p @ w_pe)
    return _layer_norm(DEEP_ALPHA * h + _moe(h, w_router, b_router, w_mlp1, b_mlp1, w_mlp2, b_mlp2) + ple,
                       ln2_g, ln2_b)


def setup_inputs(seed: int = 0) -> dict:
    key = jax.random.key(seed)
    ks = jax.random.split(key, 32)
    f32 = jnp.float32
    n_pages = PAST_LEN // PAGE_SIZE
    used = DEC_BATCH * n_pages
    n_pool = used + max(1, used // 4)

    def nrm(k, shape, scale=1.0):
        return jax.random.normal(k, shape, f32) * scale

    page_table = jax.random.permutation(ks[7], n_pool)[:used].reshape(DEC_BATCH, n_pages).astype(jnp.int32)
    return {
        'x_prompt': nrm(ks[0], (BATCH, SEQ, D_MODEL)),
        'x_sample': nrm(ks[1], (DEC_BATCH, DEC_SEQ, D_MODEL)),
        'cache_fox_k': nrm(ks[2], (DEPTH, n_pool, PAGE_SIZE, H_FOX, HEAD_DIM)),
        'cache_fox_v': nrm(ks[3], (DEPTH, n_pool, PAGE_SIZE, H_FOX, HEAD_DIM)),
        'cache_fox_logf': jax.nn.log_sigmoid(FORGET_BIAS_INIT + nrm(ks[4], (DEPTH, n_pool, PAGE_SIZE, H_FOX))),
        'cache_diff_k': nrm(ks[5], (DEPTH, n_pool, PAGE_SIZE, H_DIFF, DIFF_DIM)),
        'cache_diff_v': nrm(ks[6], (DEPTH, n_pool, PAGE_SIZE, H_DIFF, DIFF_DIM)),
        'page_table': page_table,
        'p_prompt': nrm(ks[8], (DEPTH, BATCH, SEQ, PLE_DIM)),
        'p_sample': nrm(ks[9], (DEPTH, DEC_BATCH, DEC_SEQ, PLE_DIM)),
        'w_in': nrm(ks[10], (DEPTH, D_MODEL, IN_W), D_MODEL ** -0.5),
        'b_fgate': FORGET_BIAS_INIT + nrm(ks[11], (DEPTH, H_FOX), 0.1),
        'lambda_q1': nrm(ks[12], (DEPTH, HEAD_DIM), 0.1),
        'lambda_k1': nrm(ks[13], (DEPTH, HEAD_DIM), 0.1),
        'lambda_q2': nrm(ks[14], (DEPTH, HEAD_DIM), 0.1),
        'lambda_k2': nrm(ks[15], (DEPTH, HEAD_DIM), 0.1),
        'g_subln': 1.0 + nrm(ks[16], (DEPTH, DIFF_DIM), 0.01),
        'w_o': nrm(ks[17], (DEPTH, MIX_W, D_MODEL), MIX_W ** -0.5 * DEEP_BETA),
        'ln1_g': 1.0 + nrm(ks[18], (DEPTH, D_MODEL), 0.01),
        'ln1_b': nrm(ks[19], (DEPTH, D_MODEL), 0.01),
        'w_router': nrm(ks[20], (DEPTH, D_MODEL, N_EXPERTS), D_MODEL ** -0.5),
        'b_router': nrm(ks[21], (DEPTH, N_EXPERTS), 0.01),
        'w_mlp1': nrm(ks[22], (DEPTH, N_EXPERTS, D_MODEL, 2 * D_FF), D_MODEL ** -0.5),
        'b_mlp1': nrm(ks[23], (DEPTH, N_EXPERTS, 2 * D_FF), 0.01),
        'w_mlp2': nrm(ks[24], (DEPTH, N_EXPERTS, D_FF, D_MODEL), D_FF ** -0.5 * DEEP_BETA),
        'b_mlp2': nrm(ks[25], (DEPTH, N_EXPERTS, D_MODEL), 0.01),
        'w_pe': nrm(ks[26], (DEPTH, PLE_DIM, D_MODEL), PLE_DIM ** -0.5 * DEEP_BETA),
        'w_pg': nrm(ks[27], (DEPTH, D_MODEL, D_MODEL), D_MODEL ** -0.5),
        'ln2_g': 1.0 + nrm(ks[28], (DEPTH, D_MODEL), 0.01),
        'ln2_b': nrm(ks[29], (DEPTH, D_MODEL), 0.01),
    }


def reference(x_prompt, x_sample, cache_fox_k, cache_fox_v, cache_fox_logf, cache_diff_k, cache_diff_v,
              page_table, p_prompt, p_sample, w_in, b_fgate, lambda_q1, lambda_k1, lambda_q2, lambda_k2,
              g_subln, w_o, ln1_g, ln1_b, w_router, b_router, w_mlp1, b_mlp1, w_mlp2, b_mlp2,
              w_pe, w_pg, ln2_g, ln2_b):
    xp, xs = x_prompt, x_sample
    fkp, fvp, flp, dkp, dvp = [], [], [], [], []
    fks, fvs, fls, dks, dvs = [], [], [], [], []
    for i in range(DEPTH):
        lam_init = 0.8 - 0.6 * math.exp(-0.3 * i)
        lam = _diff_lambda(lambda_q1[i], lambda_k1[i], lambda_q2[i], lambda_k2[i], lam_init)
        q_f, k_f, v_f, logf, q_d, k_d, v_d = _project(xp, w_in[i], b_fgate[i])
        o_f, o_d = _prompt_attention(q_f, k_f, v_f, logf, q_d, k_d, v_d, lam)
        xp = _finish_layer(xp, _mix_out(o_f, o_d, g_subln[i], lam_init, w_o[i]), p_prompt[i],
                           ln1_g[i], ln1_b[i], w_router[i], b_router[i], w_mlp1[i], b_mlp1[i],
                           w_mlp2[i], b_mlp2[i], w_pe[i], w_pg[i], ln2_g[i], ln2_b[i])
        fkp.append(k_f); fvp.append(v_f); flp.append(logf); dkp.append(k_d); dvp.append(v_d)
        q_f, k_f, v_f, logf, q_d, k_d, v_d = _project(xs, w_in[i], b_fgate[i])
        o_f, o_d = _sample_attention(i, q_f, k_f, v_f, logf, q_d, k_d, v_d, lam, cache_fox_k, cache_fox_v,
                                     cache_fox_logf, cache_diff_k, cache_diff_v, page_table)
        xs = _finish_layer(xs, _mix_out(o_f, o_d, g_subln[i], lam_init, w_o[i]), p_sample[i],
                           ln1_g[i], ln1_b[i], w_router[i], b_router[i], w_mlp1[i], b_mlp1[i],
                           w_mlp2[i], b_mlp2[i], w_pe[i], w_pg[i], ln2_g[i], ln2_b[i])
        fks.append(k_f); fvs.append(v_f); fls.append(logf); dks.append(k_d); dvs.append(v_d)
    return (xp, xs,
            jnp.stack(fkp), jnp.stack(fvp), jnp.stack(flp), jnp.stack(dkp), jnp.stack(dvp),
            jnp.stack(fks), jnp.stack(fvs), jnp.stack(fls), jnp.stack(dks), jnp.stack(dvs))
```

```python
import functools
import math

import jax
import jax.numpy as jnp
from jax import lax
from jax.experimental import pallas as pl
from jax.experimental.pallas import tpu as pltpu

F32 = jnp.float32
BF16 = jnp.bfloat16
I32 = jnp.int32
U32 = jnp.uint32

LANES = 128
SUBLANES = 8
VMEM_LIMIT = 56 * 1024 * 1024

TOP_K = 4
SWIGLU_ALPHA = 1.702
SWIGLU_LIMIT = 7.0
LN_EPS = 1e-5
RMS_EPS = 1e-5
NEG_INF = -1e30

TOKEN_TILE = 256
ROW_TILE = 256
ATTN_TILE = 512
DECODE_PAGES = 4


def _params(sem, vmem=VMEM_LIMIT):
    return pltpu.CompilerParams(dimension_semantics=sem, vmem_limit_bytes=vmem)


def _log_sigmoid(x):
    return jnp.minimum(x, 0.0) - jnp.log1p(jnp.exp(-jnp.abs(x)))


def _dot(a, b):
    return jnp.dot(a, b, preferred_element_type=F32)


def _dot_nt(a, b):
    return lax.dot_general(a, b, (((1,), (1,)), ((), ())), preferred_element_type=F32)


def _dot_exact_rhs(x, rhs):
    hi = x.astype(BF16)
    r1 = x - hi.astype(F32)
    mid = r1.astype(BF16)
    lo = (r1 - mid.astype(F32)).astype(BF16)
    return _dot(hi, rhs) + _dot(mid, rhs) + _dot(lo, rhs)


def _layer_norm(z, g, b):
    mu = jnp.mean(z, axis=-1, keepdims=True)
    zc = z - mu
    var = jnp.mean(zc * zc, axis=-1, keepdims=True)
    return zc * lax.rsqrt(var + LN_EPS) * g + b


def _proj_kernel(x_ref, w_ref, wfg_ref, bfg_ref, qf, kf, vf, qd, kd, vd, lf, *, width, scale):
    xb = x_ref[...].astype(BF16)
    for s, o in enumerate((qf, kf, vf, qd, kd, vd)):
        r = _dot(xb, w_ref[:, s * width:(s + 1) * width])
        if o.dtype == BF16:
            r = r * scale
        o[...] = r.astype(o.dtype)
    g = _dot(xb, wfg_ref[...]) + bfg_ref[...]
    lf[...] = _log_sigmoid(g)


def _proj(x2d, w_main, w_fg, b_fg, *, tm, width, scale):
    n, d = x2d.shape
    const = lambda i: (0, 0)
    row = lambda i: (i, 0)
    wide = lambda dt: jax.ShapeDtypeStruct((n, width), dt)
    return pl.pallas_call(
        functools.partial(_proj_kernel, width=width, scale=scale),
        grid=(n // tm,),
        in_specs=[
            pl.BlockSpec((tm, d), row),
            pl.BlockSpec(w_main.shape, const, pipeline_mode=pl.Buffered(1)),
            pl.BlockSpec(w_fg.shape, const, pipeline_mode=pl.Buffered(1)),
            pl.BlockSpec(b_fg.shape, const, pipeline_mode=pl.Buffered(1)),
        ],
        out_specs=[pl.BlockSpec((tm, width), row)] * 6 + [pl.BlockSpec((tm, LANES), row)],
        out_shape=[wide(BF16), wide(F32), wide(F32), wide(BF16), wide(F32), wide(F32),
                   jax.ShapeDtypeStruct((n, LANES), F32)],
        compiler_params=_params(("arbitrary",)),
        name="proj",
    )(x2d, w_main, w_fg, b_fg)


def _cumsum_kernel(x_ref, o_ref):
    s = x_ref.shape[-1]
    r = lax.broadcasted_iota(I32, (LANES, LANES), 0)
    c = lax.broadcasted_iota(I32, (LANES, LANES), 1)
    upper = (r <= c).astype(BF16)
    carry = jnp.zeros((x_ref.shape[0], 1), F32)
    for j in range(s // LANES):
        sl = slice(j * LANES, (j + 1) * LANES)
        blk = _dot_exact_rhs(x_ref[:, sl], upper) + carry
        o_ref[:, sl] = blk
        carry = blk[:, LANES - 1:LANES]


def _cumsum(lf_t):
    b, h, s = lf_t.shape
    return pl.pallas_call(
        _cumsum_kernel,
        grid=(b,),
        in_specs=[pl.BlockSpec((None, h, s), lambda i: (i, 0, 0))],
        out_specs=pl.BlockSpec((None, h, s), lambda i: (i, 0, 0)),
        out_shape=jax.ShapeDtypeStruct((b, h, s), F32),
        compiler_params=_params(("arbitrary",)),
        name="cumsum",
    )(lf_t)


def _softmax_step(carry, s, v):
    m, l, acc = carry
    m_new = jnp.maximum(m, jnp.max(s, axis=-1, keepdims=True))
    alpha = jnp.exp(m - m_new)
    p = jnp.exp(s - m_new)
    l = alpha * l + jnp.sum(p, axis=-1, keepdims=True)
    acc = alpha * acc + _dot(p.astype(BF16), v)
    return m_new, l, acc


def _causal_mask(t):
    r = lax.broadcasted_iota(I32, (t, t), 0)
    c = lax.broadcasted_iota(I32, (t, t), 1)
    return r >= c


def _fox_attn_kernel(q_ref, k_ref, v_ref, c_ref, o_ref, kb, vb, *, t):
    qi = pl.program_id(2)

    @pl.when(qi == 0)
    def _():
        kb[...] = k_ref[...].astype(BF16)
        vb[...] = v_ref[...].astype(BF16)

    q = q_ref[...]
    c0 = c_ref[qi][:, 0:1]

    def tile(kj, carry, masked):
        k0 = pl.multiple_of(kj * t, t)
        s = _dot_nt(q, kb[pl.ds(k0, t), :]) + (c0 - c_ref[kj])
        if masked:
            s = jnp.where(_causal_mask(t), s, NEG_INF)
        return _softmax_step(carry, s, vb[pl.ds(k0, t), :])

    d = vb.shape[-1]
    init = (jnp.full((t, 1), NEG_INF, F32), jnp.zeros((t, 1), F32), jnp.zeros((t, d), F32))
    carry = lax.fori_loop(0, qi, lambda kj, c: tile(kj, c, False), init)
    _, l, acc = tile(qi, carry, True)
    o_ref[...] = (acc / l).astype(o_ref.dtype)


def _fox_attn(q, k, v, c_rows, *, batch, seq, heads, hd, t):
    nq = seq // t
    return pl.pallas_call(
        functools.partial(_fox_attn_kernel, t=t),
        grid=(batch, heads, nq),
        in_specs=[
            pl.BlockSpec((None, t, hd), lambda b, h, i: (b, i, h)),
            pl.BlockSpec((None, seq, hd), lambda b, h, i: (b, 0, h)),
            pl.BlockSpec((None, seq, hd), lambda b, h, i: (b, 0, h)),
            pl.BlockSpec((None, nq, 1, t), lambda b, h, i: (b * heads + h, 0, 0, 0)),
        ],
        out_specs=pl.BlockSpec((None, t, hd), lambda b, h, i: (b, i, h)),
        out_shape=jax.ShapeDtypeStruct((batch, seq, heads * hd), BF16),
        scratch_shapes=[pltpu.VMEM((seq, hd), BF16), pltpu.VMEM((seq, hd), BF16)],
        compiler_params=_params(("arbitrary", "arbitrary", "arbitrary")),
        name="fox_attn",
    )(q, k, v, c_rows)


def _diff_lambda(lam_ref, lam_init):
    lq1, lk1, lq2, lk2 = (lam_ref[i:i + 1, :] for i in range(4))
    e1 = jnp.exp(jnp.sum(lq1 * lk1, axis=-1, keepdims=True))
    e2 = jnp.exp(jnp.sum(lq2 * lk2, axis=-1, keepdims=True))
    return e1 - e2 + lam_init


def _sub_ln(o, g, lam_init):
    ms = jnp.mean(o * o, axis=-1, keepdims=True)
    return o * lax.rsqrt(ms + RMS_EPS) * g * (1.0 - lam_init)


def _diff_attn_kernel(slope_ref, q_ref, k_ref, v_ref, lam_ref, g_ref, o_ref, kb, vb, *, t, hd, lam_init):
    h = pl.program_id(1)
    qi = pl.program_id(2)

    @pl.when(qi == 0)
    def _():
        kb[...] = k_ref[...].astype(BF16)
        vb[...] = v_ref[...].astype(BF16)

    q = q_ref[...]
    q1, q2 = q[:, :hd], q[:, hd:]
    slope = slope_ref[h]
    col = lax.broadcasted_iota(I32, (1, t), 1)

    def tile(kj, carry, masked):
        c1, c2 = carry
        k0 = pl.multiple_of(kj * t, t)
        k = kb[pl.ds(k0, t), :]
        v = vb[pl.ds(k0, t), :]
        bias = slope * (col - (qi - kj) * t).astype(F32)
        s1 = _dot_nt(q1, k[:, :hd]) + bias
        s2 = _dot_nt(q2, k[:, hd:]) + bias
        if masked:
            mask = _causal_mask(t)
            s1 = jnp.where(mask, s1, NEG_INF)
            s2 = jnp.where(mask, s2, NEG_INF)
        return _softmax_step(c1, s1, v), _softmax_step(c2, s2, v)

    d = vb.shape[-1]
    one = (jnp.full((t, 1), NEG_INF, F32), jnp.zeros((t, 1), F32), jnp.zeros((t, d), F32))
    carry = lax.fori_loop(0, qi, lambda kj, c: tile(kj, c, False), (one, one))
    (_, l1, a1), (_, l2, a2) = tile(qi, carry, True)
    lam = _diff_lambda(lam_ref, lam_init)
    o = a1 / l1 - lam * (a2 / l2)
    o_ref[...] = _sub_ln(o, g_ref[...], lam_init).astype(o_ref.dtype)


def _diff_attn(slopes, q, k, v, lam_rows, g_subln, *, batch, seq, heads, hd, t, lam_init):
    nq = seq // t
    dd = 2 * hd
    return pl.pallas_call(
        functools.partial(_diff_attn_kernel, t=t, hd=hd, lam_init=lam_init),
        grid=(batch, heads, nq),
        in_specs=[
            pl.BlockSpec(memory_space=pltpu.SMEM),
            pl.BlockSpec((None, t, dd), lambda b, h, i: (b, i, h)),
            pl.BlockSpec((None, seq, dd), lambda b, h, i: (b, 0, h)),
            pl.BlockSpec((None, seq, dd), lambda b, h, i: (b, 0, h)),
            pl.BlockSpec(lam_rows.shape, lambda b, h, i: (0, 0)),
            pl.BlockSpec(g_subln.shape, lambda b, h, i: (0, 0)),
        ],
        out_specs=pl.BlockSpec((None, t, dd), lambda b, h, i: (b, i, h)),
        out_shape=jax.ShapeDtypeStruct((batch, seq, heads * dd), BF16),
        scratch_shapes=[pltpu.VMEM((seq, dd), BF16), pltpu.VMEM((seq, dd), BF16)],
        compiler_params=_params(("arbitrary", "arbitrary", "arbitrary")),
        name="diff_attn",
    )(slopes, q, k, v, lam_rows, g_subln)


def _decay_pre_kernel(x_ref, dec_ref, tot_ref):
    g, h, p = x_ref.shape
    x = x_ref[...].reshape(g * h, p)
    r = lax.broadcasted_iota(I32, (p, p), 0)
    c = lax.broadcasted_iota(I32, (p, p), 1)
    later = (r > c).astype(BF16)
    ones = jnp.ones((p, p), BF16)
    dec_ref[...] = _dot_exact_rhs(x, later).reshape(g, h, p)
    tot_ref[...] = _dot_exact_rhs(x, ones).reshape(g, h, p)


def _decay_pre(lf_pages, *, group):
    n, h, p = lf_pages.shape
    spec = pl.BlockSpec((group, h, p), lambda i: (i, 0, 0))
    shape = jax.ShapeDtypeStruct((n, h, p), F32)
    return pl.pallas_call(
        _decay_pre_kernel,
        grid=(n // group,),
        in_specs=[spec],
        out_specs=[spec, spec],
        out_shape=[shape, shape],
        compiler_params=_params(("arbitrary",)),
        name="decay_pre",
    )(lf_pages)


def _block_diag_rows(q, n):
    qt = jnp.concatenate([q] * n, axis=-1)
    row = lax.broadcasted_iota(I32, qt.shape, 0)
    blk = lax.broadcasted_iota(I32, qt.shape, 1) // LANES
    return jnp.where(row == blk, qt, jnp.zeros_like(qt))


def _decode_update(m_s, l_s, acc_s, scores, values):
    s = jnp.concatenate(scores, axis=-1)
    m_old = m_s[...]
    m_new = jnp.maximum(m_old, jnp.max(s, axis=-1, keepdims=True))
    alpha = jnp.exp(m_old - m_new)
    p = jnp.exp(s - m_new)
    l_s[...] = alpha * l_s[...] + jnp.sum(p, axis=-1, keepdims=True)
    page = scores[0].shape[-1]
    pv = _dot(p[:, :page].astype(BF16), values[0])
    for g in range(1, len(values)):
        pv = pv + _dot(p[:, g * page:(g + 1) * page].astype(BF16), values[g])
    acc_s[...] = alpha * acc_s[...] + pv
    m_s[...] = m_new


def _decode_init(m_s, l_s, acc_s):
    m_s[...] = jnp.full(m_s.shape, NEG_INF, F32)
    l_s[...] = jnp.zeros(l_s.shape, F32)
    acc_s[...] = jnp.zeros(acc_s.shape, F32)


def _round_bf16(x):
    return x.astype(BF16).astype(F32)


def _fox_decode_kernel(pt_ref, q_ref, kn_ref, vn_ref, lfn_ref, *refs, n_pages, heads):
    gp = n_pages
    k_refs, v_refs = refs[:gp], refs[gp:2 * gp]
    dec_refs, tot_refs = refs[2 * gp:3 * gp], refs[3 * gp:4 * gp]
    o_ref = refs[4 * gp]
    m_s, l_s, acc_s, run_s = refs[4 * gp + 1:]
    j = pl.program_id(1)

    @pl.when(j == 0)
    def _():
        _decode_init(m_s, l_s, acc_s)
        run_s[...] = jnp.broadcast_to(lfn_ref[...], run_s.shape)

    q = q_ref[...]
    qblk = _block_diag_rows(q, heads)
    run = run_s[...]
    scores, values = [], []
    for g in range(gp):
        kp = jnp.concatenate([k_refs[g][pl.ds(h, LANES, stride=heads), :] for h in range(heads)],
                             axis=-1).astype(BF16)
        vp = jnp.concatenate([v_refs[g][pl.ds(h, LANES, stride=heads), :] for h in range(heads)],
                             axis=-1).astype(BF16)
        scores.append(_dot_nt(qblk, kp) + dec_refs[g][...] + run)
        values.append(vp)
        run = run + tot_refs[g][...]
    run_s[...] = run
    _decode_update(m_s, l_s, acc_s, scores, values)

    @pl.when(j == pl.num_programs(1) - 1)
    def _():
        hd = q.shape[-1]
        acc = acc_s[...]
        row = lax.broadcasted_iota(I32, (heads, hd), 0)
        o = jnp.zeros((heads, hd), F32)
        for h in range(heads):
            o = o + jnp.where(row == h, acc[:, h * hd:(h + 1) * hd], 0.0)
        s_new = jnp.sum(q.astype(F32) * _round_bf16(kn_ref[...]), axis=-1, keepdims=True)
        m_old = m_s[...]
        m_new = jnp.maximum(m_old, s_new)
        alpha = jnp.exp(m_old - m_new)
        p_new = jnp.exp(s_new - m_new)
        l = alpha * l_s[...] + p_new
        o = alpha * o + _round_bf16(p_new) * _round_bf16(vn_ref[...])
        o_ref[...] = o / l


def _fox_decode(page_table, q, k_new, v_new, lf_new, cache_k, cache_v, dec, tot, *, heads, hd, page):
    nb, npg = page_table.shape
    gp = DECODE_PAGES
    rows = page * heads

    def kv_spec(g):
        return pl.BlockSpec((rows, hd), lambda b, j, pt: (pt[b, npg - 1 - (j * gp + g)], 0))

    def dec_spec(g):
        return pl.BlockSpec((None, heads, page), lambda b, j, pt: (pt[b, npg - 1 - (j * gp + g)], 0, 0))

    per_seq = lambda shape: pl.BlockSpec((None,) + shape, lambda b, j, pt: (b, 0, 0))
    return pl.pallas_call(
        functools.partial(_fox_decode_kernel, n_pages=gp, heads=heads),
        grid_spec=pltpu.PrefetchScalarGridSpec(
            num_scalar_prefetch=1,
            grid=(nb, npg // gp),
            in_specs=[per_seq((heads, hd)), per_seq((heads, hd)), per_seq((heads, hd)), per_seq((heads, 1))]
            + [kv_spec(g) for g in range(gp)] * 2 + [dec_spec(g) for g in range(gp)] * 2,
            out_specs=per_seq((heads, hd)),
            scratch_shapes=[pltpu.VMEM((heads, 1), F32), pltpu.VMEM((heads, 1), F32),
                            pltpu.VMEM((heads, heads * hd), F32), pltpu.VMEM((heads, page), F32)],
        ),
        out_shape=jax.ShapeDtypeStruct((nb, heads, hd), F32),
        compiler_params=_params(("arbitrary", "arbitrary")),
        name="fox_decode",
    )(page_table, q, k_new, v_new, lf_new, *([cache_k] * gp), *([cache_v] * gp), *([dec] * gp), *([tot] * gp))


def _diff_decode_kernel(pt_ref, q_ref, kn_ref, vn_ref, slope_ref, lam_ref, g_ref, *refs,
                        n_pages, heads, hd, past, lam_init):
    gp = n_pages
    k_refs, v_refs = refs[:gp], refs[gp:2 * gp]
    o_ref = refs[2 * gp]
    m_s, l_s, acc_s = refs[2 * gp + 1:]
    j = pl.program_id(1)
    nv = 2 * heads
    page = k_refs[0].shape[0]

    @pl.when(j == 0)
    def _():
        _decode_init(m_s, l_s, acc_s)

    q = q_ref[...]
    qblk = _block_diag_rows(q, nv)
    slope = slope_ref[...]
    lane = lax.broadcasted_iota(I32, (nv, page), 1)
    scores, values = [], []
    for g in range(gp):
        kp = jnp.concatenate([k_refs[g][:, h, :] for h in range(heads)], axis=-1).astype(BF16)
        vp = jnp.concatenate([v_refs[g][:, h, :] for h in range(heads)], axis=-1).astype(BF16)
        pos = (j * gp + g) * page + lane
        scores.append(_dot_nt(qblk, kp) - slope * (past - pos).astype(F32))
        values.append(vp)
    _decode_update(m_s, l_s, acc_s, scores, values)

    @pl.when(j == pl.num_programs(1) - 1)
    def _():
        dd = 2 * hd
        acc = acc_s[...]
        s_new = jnp.sum(q.astype(F32) * _round_bf16(kn_ref[...]), axis=-1, keepdims=True)
        m_old = m_s[...]
        m_new = jnp.maximum(m_old, s_new)
        alpha = jnp.exp(m_old - m_new)
        p_new = jnp.exp(s_new - m_new)
        l = alpha * l_s[...] + p_new
        pb = _round_bf16(p_new)
        vn = _round_bf16(vn_ref[...])
        lam = _diff_lambda(lam_ref, lam_init)
        outs = []
        for h in range(heads):
            blk = acc[:, h * dd:(h + 1) * dd]
            r1, r2 = 2 * h, 2 * h + 1
            o1 = (alpha[r1:r1 + 1] * blk[r1:r1 + 1] + pb[r1:r1 + 1] * vn[h:h + 1]) / l[r1:r1 + 1]
            o2 = (alpha[r2:r2 + 1] * blk[r2:r2 + 1] + pb[r2:r2 + 1] * vn[h:h + 1]) / l[r2:r2 + 1]
            outs.append(_sub_ln(o1 - lam * o2, g_ref[...], lam_init))
        o_ref[...] = jnp.concatenate(outs, axis=-1)


def _diff_decode(page_table, q, k_new, v_new, slope_rows, lam_rows, g_subln, cache_k, cache_v,
                 *, heads, hd, page, lam_init):
    nb, npg = page_table.shape
    gp = DECODE_PAGES
    dd = 2 * hd
    nv = 2 * heads

    def kv_spec(g):
        return pl.BlockSpec((None, page, heads, dd), lambda b, j, pt: (pt[b, j * gp + g], 0, 0, 0))

    per_seq = lambda shape: pl.BlockSpec((None,) + shape, lambda b, j, pt: (b, 0, 0))
    whole = lambda a: pl.BlockSpec(a.shape, lambda b, j, pt: (0, 0))
    return pl.pallas_call(
        functools.partial(_diff_decode_kernel, n_pages=gp, heads=heads, hd=hd,
                          past=npg * page, lam_init=lam_init),
        grid_spec=pltpu.PrefetchScalarGridSpec(
            num_scalar_prefetch=1,
            grid=(nb, npg // gp),
            in_specs=[per_seq((nv, hd)), per_seq((nv, hd)), per_seq((heads, dd)),
                      whole(slope_rows), whole(lam_rows), whole(g_subln)]
            + [kv_spec(g) for g in range(gp)] * 2,
            out_specs=per_seq((1, heads * dd)),
            scratch_shapes=[pltpu.VMEM((nv, 1), F32), pltpu.VMEM((nv, 1), F32),
                            pltpu.VMEM((nv, heads * dd), F32)],
        ),
        out_shape=jax.ShapeDtypeStruct((nb, 1, heads * dd), F32),
        compiler_params=_params(("arbitrary", "arbitrary")),
        name="diff_decode",
    )(page_table, q, k_new, v_new, slope_rows, lam_rows, g_subln, *([cache_k] * gp), *([cache_v] * gp))


def _pack_bf16_pairs(lo, hi):
    lo_bits = lax.bitcast_convert_type(_round_bf16(lo), U32) >> 16
    hi_bits = lax.bitcast_convert_type(_round_bf16(hi), U32) & jnp.uint32(0xFFFF0000)
    return lo_bits | hi_bits


def _unpack_bf16_pairs(w):
    lo = lax.bitcast_convert_type(w << 16, F32).astype(BF16)
    hi = lax.bitcast_convert_type(w & jnp.uint32(0xFFFF0000), F32).astype(BF16)
    return lo, hi


def _post_kernel(xp_ref, ofp_ref, odp_ref, pp_ref, xs_ref, ofs_ref, ods_ref, ps_ref,
                 wo_ref, wpg_ref, wpe_ref, wr_ref, br_ref, g1_ref, b1_ref,
                 h3_ref, base_ref, eidx_ref, wts_ref, cnt_ref, *, alpha, n_experts):
    i = pl.program_id(0)
    is_sample = i == pl.num_programs(0) - 1
    pick = lambda a, b: jnp.where(is_sample, a[...], b[...])
    x = pick(xs_ref, xp_ref)
    of = pick(ofs_ref, ofp_ref)
    od = pick(ods_ref, odp_ref)
    p = pick(ps_ref, pp_ref)
    tm, d = x.shape
    half = of.shape[-1]

    mix = _dot(of, wo_ref[:half, :]) + _dot(od, wo_ref[half:, :])
    h = _layer_norm(alpha * x + mix, g1_ref[...], b1_ref[...])
    hb = h.astype(BF16)
    ple = jax.nn.sigmoid(_dot(hb, wpg_ref[...])) * _dot(p.astype(BF16), wpe_ref[...])
    base_ref[...] = alpha * h + ple

    nw = d // (2 * LANES)
    for r in range(nw):
        lo = h[:, r * LANES:(r + 1) * LANES]
        hi = h[:, (r + nw) * LANES:(r + nw + 1) * LANES]
        h3_ref[pl.ds(r, tm, stride=nw), :] = _pack_bf16_pairs(lo, hi)

    logits = _dot(hb, wr_ref[...]) + br_ref[...]
    lane = lax.broadcasted_iota(I32, logits.shape, 1)
    lane_f = lane.astype(F32)
    work = logits
    vals, idxs = [], []
    for _ in range(TOP_K):
        mx = jnp.max(work, axis=-1, keepdims=True)
        ix = jnp.min(jnp.where(work == mx, lane_f, float(LANES)), axis=-1, keepdims=True)
        vals.append(mx)
        idxs.append(ix)
        work = jnp.where(lane_f == ix, -jnp.inf, work)
    exps = [jnp.exp(v - vals[0]) for v in vals]
    denom = exps[0]
    for e in exps[1:]:
        denom = denom + e
    eidx = jnp.zeros(logits.shape, F32)
    wts = jnp.zeros(logits.shape, F32)
    sel = jnp.zeros(logits.shape, F32)
    for k in range(TOP_K):
        eidx = jnp.where(lane == k, idxs[k], eidx)
        wts = jnp.where(lane == k, exps[k] / denom, wts)
        sel = sel + (lane_f == idxs[k]).astype(F32)
    eidx_ref[...] = eidx.astype(I32)
    wts_ref[...] = wts
    cnt_ref[...] = jnp.sum(sel, axis=0, keepdims=True)


def _post(prompt, sample, weights, *, n_tiles, tm, alpha, n_experts):
    xp, ofp, odp, pp = prompt
    xs, ofs, ods, ps = sample
    last = n_tiles - 2
    pmap = lambda i: (jnp.minimum(i, last), 0)
    smap = lambda i: (0, 0)
    const = lambda i: (0, 0)
    d = xp.shape[-1]
    n_tot = n_tiles * tm
    nw = d // (2 * LANES)
    in_specs = [pl.BlockSpec((tm, a.shape[-1]), pmap) for a in prompt]
    in_specs += [pl.BlockSpec((tm, a.shape[-1]), smap) for a in sample]
    in_specs += [pl.BlockSpec(w.shape, const, pipeline_mode=pl.Buffered(1)) for w in weights]
    return pl.pallas_call(
        functools.partial(_post_kernel, alpha=alpha, n_experts=n_experts),
        grid=(n_tiles,),
        in_specs=in_specs,
        out_specs=[
            pl.BlockSpec((tm * nw, LANES), lambda i: (i, 0)),
            pl.BlockSpec((tm, d), lambda i: (i, 0)),
            pl.BlockSpec((tm, LANES), lambda i: (i, 0)),
            pl.BlockSpec((tm, LANES), lambda i: (i, 0)),
            pl.BlockSpec((None, 1, LANES), lambda i: (i, 0, 0)),
        ],
        out_shape=[
            jax.ShapeDtypeStruct((n_tot * nw, LANES), U32),
            jax.ShapeDtypeStruct((n_tot, d), F32),
            jax.ShapeDtypeStruct((n_tot, LANES), I32),
            jax.ShapeDtypeStruct((n_tot, LANES), F32),
            jax.ShapeDtypeStruct((n_tiles, 1, LANES), F32),
        ],
        compiler_params=_params(("arbitrary",)),
        name="post",
    )(*prompt, *sample, *weights)


def _rank_kernel(eidx_ref, start_ref, dest_ref):
    eidx = eidx_ref[...]
    tm = eidx.shape[0]
    lane = lax.broadcasted_iota(I32, eidx.shape, 1)
    picks = [lane == eidx[:, k:k + 1] for k in range(TOP_K)]
    sel = picks[0]
    for pk in picks[1:]:
        sel = sel | pk
    r = lax.broadcasted_iota(I32, (tm, tm), 0)
    c = lax.broadcasted_iota(I32, (tm, tm), 1)
    before = (r > c).astype(BF16)
    pos = _dot(before, sel.astype(BF16)) + start_ref[...]
    dest = jnp.zeros(eidx.shape, F32)
    for k in range(TOP_K):
        dk = jnp.sum(jnp.where(picks[k], pos, 0.0), axis=-1, keepdims=True)
        dest = jnp.where(lane == k, dk, dest)
    dest_ref[...] = dest.astype(I32)


def _rank(eidx, start, *, tm):
    n_tot = eidx.shape[0]
    return pl.pallas_call(
        _rank_kernel,
        grid=(n_tot // tm,),
        in_specs=[pl.BlockSpec((tm, LANES), lambda i: (i, 0)),
                  pl.BlockSpec((None, 1, LANES), lambda i: (i, 0, 0))],
        out_specs=pl.BlockSpec((tm, LANES), lambda i: (i, 0)),
        out_shape=jax.ShapeDtypeStruct((n_tot, LANES), I32),
        compiler_params=_params(("arbitrary",)),
        name="rank",
    )(eidx, start)


def _dispatch_kernel(dest_ref, h3_ref, xs_in_ref, xs_ref, idx_s, sem_idx, sem, *, tm, nw):
    del xs_in_ref
    i = pl.program_id(0)
    cp = pltpu.make_async_copy(dest_ref, idx_s, sem_idx)
    cp.start()
    cp.wait()

    def row_copy(t, k):
        src = h3_ref.at[pl.ds(pl.multiple_of((i * tm + t) * nw, nw), nw), :]
        dst = xs_ref.at[pl.ds(pl.multiple_of(idx_s[t * TOP_K + k] * nw, nw), nw), :]
        return pltpu.make_async_copy(src, dst, sem)

    def issue(t, carry):
        for k in range(TOP_K):
            row_copy(t, k).start()
        return carry

    def drain(t, carry):
        for k in range(TOP_K):
            row_copy(t, k).wait()
        return carry

    lax.fori_loop(0, tm, issue, 0)
    lax.fori_loop(0, tm, drain, 0)


def _dispatch(dest_flat, h3, xs_init, *, tm, nw):
    n_tiles = dest_flat.shape[0] // (tm * TOP_K)
    return pl.pallas_call(
        functools.partial(_dispatch_kernel, tm=tm, nw=nw),
        grid=(n_tiles,),
        in_specs=[pl.BlockSpec((tm * TOP_K,), lambda i: (i,)),
                  pl.BlockSpec(memory_space=pl.ANY),
                  pl.BlockSpec(memory_space=pl.ANY)],
        out_specs=pl.BlockSpec(memory_space=pl.ANY),
        out_shape=jax.ShapeDtypeStruct(xs_init.shape, xs_init.dtype),
        scratch_shapes=[pltpu.SMEM((tm * TOP_K,), I32), pltpu.SemaphoreType.DMA(()), pltpu.SemaphoreType.DMA(())],
        input_output_aliases={2: 0},
        compiler_params=_params(("arbitrary",)),
        name="dispatch",
    )(dest_flat, h3, xs_init)


def _mlp1_kernel(be_ref, nu_ref, x_ref, wg_ref, wl_ref, bg_ref, bl_ref, o_ref, *, rows, nw):
    r = pl.program_id(1)

    @pl.when(r < nu_ref[0])
    def _():
        cols = [None] * (2 * nw)
        for w in range(nw):
            cols[w], cols[w + nw] = _unpack_bf16_pairs(x_ref[pl.ds(w, rows, stride=nw), :])
        xb = jnp.concatenate(cols, axis=-1)
        glu = jnp.minimum(_dot(xb, wg_ref[...]) + bg_ref[...], SWIGLU_LIMIT)
        lin = jnp.clip(_dot(xb, wl_ref[...]) + bl_ref[...], -SWIGLU_LIMIT, SWIGLU_LIMIT)
        a = glu * jax.nn.sigmoid(SWIGLU_ALPHA * glu) * (lin + 1.0)
        o_ref[...] = a.astype(o_ref.dtype)

    @pl.when(r >= nu_ref[0])
    def _():
        o_ref[...] = jnp.zeros(o_ref.shape, o_ref.dtype)


def _mlp1(block_expert, n_used, xs, w_glu, w_lin, b_glu, b_lin, *, rows, nw, tf):
    n_blocks = block_expert.shape[0]
    e, d, dff = w_glu.shape
    wspec = pl.BlockSpec((None, d, tf), lambda j, r, be, nu: (be[r], 0, j))
    bspec = pl.BlockSpec((None, 1, tf), lambda j, r, be, nu: (be[r], 0, j))
    return pl.pallas_call(
        functools.partial(_mlp1_kernel, rows=rows, nw=nw),
        grid_spec=pltpu.PrefetchScalarGridSpec(
            num_scalar_prefetch=2,
            grid=(dff // tf, n_blocks),
            in_specs=[pl.BlockSpec((rows * nw, LANES), lambda j, r, be, nu: (r, 0)),
                      wspec, wspec, bspec, bspec],
            out_specs=pl.BlockSpec((rows, tf), lambda j, r, be, nu: (r, j)),
        ),
        out_shape=jax.ShapeDtypeStruct((n_blocks * rows, dff), BF16),
        compiler_params=_params(("arbitrary", "arbitrary")),
        name="mlp1",
    )(block_expert, n_used, xs, w_glu, w_lin, b_glu, b_lin)


def _mlp2_kernel(be_ref, nu_ref, a_ref, w_ref, b_ref, y3_ref, *, rows):
    r = pl.program_id(0)

    @pl.when(r < nu_ref[0])
    def _():
        y = _dot(a_ref[...], w_ref[...]) + b_ref[...]
        nc = y.shape[-1] // LANES
        for c in range(nc):
            y3_ref[pl.ds(c, rows, stride=nc), :] = y[:, c * LANES:(c + 1) * LANES]

    @pl.when(r >= nu_ref[0])
    def _():
        y3_ref[...] = jnp.zeros(y3_ref.shape, y3_ref.dtype)


def _mlp2(block_expert, n_used, act, w2, b2, *, rows):
    n_blocks = block_expert.shape[0]
    e, dff, d = w2.shape
    nc = d // LANES
    return pl.pallas_call(
        functools.partial(_mlp2_kernel, rows=rows),
        grid_spec=pltpu.PrefetchScalarGridSpec(
            num_scalar_prefetch=2,
            grid=(n_blocks,),
            in_specs=[pl.BlockSpec((rows, dff), lambda r, be, nu: (r, 0)),
                      pl.BlockSpec((None, dff, d), lambda r, be, nu: (be[r], 0, 0)),
                      pl.BlockSpec((None, 1, d), lambda r, be, nu: (be[r], 0, 0))],
            out_specs=pl.BlockSpec((rows * nc, LANES), lambda r, be, nu: (r, 0)),
        ),
        out_shape=jax.ShapeDtypeStruct((n_blocks * rows * nc, LANES), F32),
        compiler_params=_params(("arbitrary",)),
        name="mlp2",
    )(block_expert, n_used, act, w2, b2)


def _combine_kernel(dest_ref, wts_ref, base_ref, g_ref, b_ref, y3_ref, yp_ref, ys_ref,
                    idx_s, ybuf, sem_idx, sem, *, tm, nc):
    i = pl.program_id(0)
    cp = pltpu.make_async_copy(dest_ref, idx_s, sem_idx)
    cp.start()
    cp.wait()

    def row_copy(t, k):
        src = y3_ref.at[pl.ds(pl.multiple_of(idx_s[t * TOP_K + k] * nc, nc), nc), :]
        dst = ybuf.at[pl.ds(pl.multiple_of((k * tm + t) * nc, nc), nc), :]
        return pltpu.make_async_copy(src, dst, sem)

    def issue(t, carry):
        for k in range(TOP_K):
            row_copy(t, k).start()
        return carry

    def drain(t, carry):
        for k in range(TOP_K):
            row_copy(t, k).wait()
        return carry

    lax.fori_loop(0, tm, issue, 0)
    lax.fori_loop(0, tm, drain, 0)

    wts = wts_ref[...]
    wk = [jnp.broadcast_to(wts[:, k:k + 1], (tm, LANES)) for k in range(TOP_K)]
    chunks = []
    for c in range(nc):
        acc = wk[0] * ybuf[pl.ds(c, tm, stride=nc), :]
        for k in range(1, TOP_K):
            acc = acc + wk[k] * ybuf[pl.ds(k * tm * nc + c, tm, stride=nc), :]
        chunks.append(acc)
    moe = jnp.concatenate(chunks, axis=-1)
    out = _layer_norm(base_ref[...] + moe, g_ref[...], b_ref[...])
    is_sample = i == pl.num_programs(0) - 1

    @pl.when(jnp.logical_not(is_sample))
    def _():
        yp_ref[...] = out

    @pl.when(is_sample)
    def _():
        ys_ref[...] = out


def _combine(dest_flat, wts, base, ln_g, ln_b, y3, *, tm, n_prompt):
    n_tot, d = base.shape
    n_tiles = n_tot // tm
    nc = d // LANES
    last = n_tiles - 2
    const = lambda i: (0, 0)
    return pl.pallas_call(
        functools.partial(_combine_kernel, tm=tm, nc=nc),
        grid=(n_tiles,),
        in_specs=[pl.BlockSpec((tm * TOP_K,), lambda i: (i,)),
                  pl.BlockSpec((tm, LANES), lambda i: (i, 0)),
                  pl.BlockSpec((tm, d), lambda i: (i, 0)),
                  pl.BlockSpec(ln_g.shape, const),
                  pl.BlockSpec(ln_b.shape, const),
                  pl.BlockSpec(memory_space=pl.ANY)],
        out_specs=[pl.BlockSpec((tm, d), lambda i: (jnp.minimum(i, last), 0)),
                   pl.BlockSpec((tm, d), const)],
        out_shape=[jax.ShapeDtypeStruct((n_prompt, d), F32), jax.ShapeDtypeStruct((tm, d), F32)],
        scratch_shapes=[pltpu.SMEM((tm * TOP_K,), I32), pltpu.VMEM((TOP_K * tm * nc, LANES), F32),
                        pltpu.SemaphoreType.DMA(()), pltpu.SemaphoreType.DMA(())],
        compiler_params=_params(("arbitrary",)),
        name="combine",
    )(dest_flat, wts, base, ln_g, ln_b, y3)


def kernel(x_prompt, x_sample, cache_fox_k, cache_fox_v, cache_fox_logf, cache_diff_k, cache_diff_v, page_table, p_prompt, p_sample, w_in, b_fgate, lambda_q1, lambda_k1, lambda_q2, lambda_k2, g_subln, w_o, ln1_g, ln1_b, w_router, b_router, w_mlp1, b_mlp1, w_mlp2, b_mlp2, w_pe, w_pg, ln2_g, ln2_b):
    depth = w_in.shape[0]
    assert depth == 1, "single-layer step only"
    batch, seq, d_model = x_prompt.shape
    dec_batch, dec_seq, _ = x_sample.shape
    assert dec_seq == 1
    _, n_pool, page, h_fox, hd = cache_fox_k.shape
    _, _, _, h_diff, dd = cache_diff_k.shape
    assert hd == LANES and dd == 2 * hd and h_fox == SUBLANES and 2 * h_diff == SUBLANES
    fox_w, diff_w = h_fox * hd, h_diff * dd
    assert fox_w == diff_w
    n_experts = w_router.shape[-1]
    d_ff = w_mlp2.shape[2]
    n_prompt = batch * seq
    lam_init = 0.8 - 0.6 * math.exp(-0.3 * 0)
    deep_alpha = (2 * depth) ** 0.25
    scale = hd ** -0.5
    tm = TOKEN_TILE

    wi = w_in[0]
    o3, o4 = 3 * fox_w, 3 * fox_w + h_fox
    w_main = jnp.concatenate([wi[:, :o3], wi[:, o4:]], axis=1).astype(BF16)
    w_fg = jnp.pad(wi[:, o3:o4], ((0, 0), (0, LANES - h_fox))).astype(BF16)
    b_fg = jnp.pad(b_fgate[0], (0, LANES - h_fox)).reshape(1, LANES)
    lam_rows = jnp.stack([lambda_q1[0], lambda_k1[0], lambda_q2[0], lambda_k2[0]])
    g_sub = g_subln[0].reshape(1, dd)
    slopes = 2.0 ** (-8.0 * jnp.arange(1, h_diff + 1, dtype=F32) / h_diff)

    xp2 = x_prompt.reshape(n_prompt, d_model)
    qf, kf, vf, qd, kd, vd, lf = _proj(xp2, w_main, w_fg, b_fg, tm=tm, width=fox_w, scale=scale)
    logf_p = lf[:, :h_fox]
    lf_t = jnp.swapaxes(logf_p.reshape(batch, seq, h_fox), 1, 2)
    t = ATTN_TILE
    c_rows = _cumsum(lf_t).reshape(batch * h_fox, seq // t, 1, t)
    b3 = lambda a: a.reshape(batch, seq, a.shape[-1])
    o_f = _fox_attn(b3(qf), b3(kf), b3(vf), c_rows, batch=batch, seq=seq, heads=h_fox, hd=hd, t=t)
    o_d = _diff_attn(slopes, b3(qd), b3(kd), b3(vd), lam_rows, g_sub,
                     batch=batch, seq=seq, heads=h_diff, hd=hd, t=t, lam_init=lam_init)

    xs2 = x_sample.reshape(dec_batch, d_model)
    qfs, kfs, vfs, qds, kds, vds, lfs = _proj(xs2, w_main, w_fg, b_fg, tm=dec_batch, width=fox_w, scale=scale)
    logf_s = lfs[:, :h_fox]
    lf_pages = jnp.swapaxes(cache_fox_logf[0], 1, 2)
    dec, tot = _decay_pre(lf_pages, group=64)
    o_fs = _fox_decode(page_table, qfs.reshape(dec_batch, h_fox, hd), kfs.reshape(dec_batch, h_fox, hd),
                       vfs.reshape(dec_batch, h_fox, hd), logf_s.reshape(dec_batch, h_fox, 1),
                       cache_fox_k.reshape(n_pool * page * h_fox, hd),
                       cache_fox_v.reshape(n_pool * page * h_fox, hd), dec, tot,
                       heads=h_fox, hd=hd, page=page)
    slope_rows = jnp.broadcast_to(jnp.repeat(slopes, 2)[:, None], (2 * h_diff, page))
    o_ds = _diff_decode(page_table, qds.reshape(dec_batch, 2 * h_diff, hd), kds.reshape(dec_batch, 2 * h_diff, hd),
                        vds.reshape(dec_batch, h_diff, dd), slope_rows, lam_rows, g_sub,
                        cache_diff_k.reshape(n_pool, page, h_diff, dd),
                        cache_diff_v.reshape(n_pool, page, h_diff, dd),
                        heads=h_diff, hd=hd, page=page, lam_init=lam_init)

    pad_rows = lambda a: jnp.pad(a, ((0, tm - a.shape[0]), (0, 0)))
    prompt_in = (xp2, o_f.reshape(n_prompt, fox_w), o_d.reshape(n_prompt, diff_w),
                 p_prompt[0].reshape(n_prompt, -1))
    sample_in = (pad_rows(xs2), pad_rows(o_fs.reshape(dec_batch, fox_w).astype(BF16)),
                 pad_rows(o_ds.reshape(dec_batch, diff_w).astype(BF16)),
                 pad_rows(p_sample[0].reshape(dec_batch, -1)))
    w_r = jnp.pad(w_router[0], ((0, 0), (0, LANES - n_experts))).astype(BF16)
    b_r = jnp.pad(b_router[0], (0, LANES - n_experts), constant_values=NEG_INF).reshape(1, LANES)
    weights = (w_o[0].astype(BF16), w_pg[0].astype(BF16), w_pe[0].astype(BF16), w_r, b_r,
               ln1_g[0].reshape(1, d_model), ln1_b[0].reshape(1, d_model))
    n_tiles = n_prompt // tm + 1
    h3, base, eidx, wts, cnt = _post(prompt_in, sample_in, weights, n_tiles=n_tiles, tm=tm,
                                     alpha=deep_alpha, n_experts=n_experts)

    rows = ROW_TILE
    cnt = cnt[:, 0, :].astype(I32)
    totals = jnp.sum(cnt, axis=0)
    padded = ((totals + rows - 1) // rows) * rows
    group_end = jnp.cumsum(padded)
    group_off = group_end - padded
    start = group_off[None, :] + jnp.cumsum(cnt, axis=0) - cnt
    n_tot = n_tiles * tm
    n_blocks = (n_tot * TOP_K) // rows + n_experts
    blk_row = jnp.arange(n_blocks, dtype=I32) * rows
    block_expert = jnp.minimum(jnp.sum(blk_row[:, None] >= group_end[None, :n_experts], axis=1),
                               n_experts - 1).astype(I32)
    n_used = (group_end[n_experts - 1] // rows).astype(I32).reshape(1)
    dest = _rank(eidx, start.astype(F32).reshape(n_tiles, 1, LANES), tm=tm)
    dest_flat = dest[:, :TOP_K].reshape(n_tot * TOP_K)

    nw = d_model // (2 * LANES)
    xs_init = jnp.zeros((n_blocks * rows * nw, LANES), U32)
    xs = _dispatch(dest_flat, h3, xs_init, tm=tm, nw=nw)
    w1 = w_mlp1[0]
    b1 = b_mlp1[0]
    w_glu, w_lin = w1[..., 0::2].astype(BF16), w1[..., 1::2].astype(BF16)
    b_glu, b_lin = b1[:, None, 0::2], b1[:, None, 1::2]
    act = _mlp1(block_expert, n_used, xs, w_glu, w_lin, b_glu, b_lin, rows=rows, nw=nw, tf=1024)
    y3 = _mlp2(block_expert, n_used, act, w_mlp2[0].astype(BF16), b_mlp2[0][:, None, :], rows=rows)
    y_p, y_s = _combine(dest_flat, wts, base, ln2_g[0].reshape(1, d_model), ln2_b[0].reshape(1, d_model), y3,
                        tm=tm, n_prompt=n_prompt)

    p5 = lambda a, h, w: a.reshape(1, batch, seq, h, w)
    s5 = lambda a, h, w: a.reshape(1, dec_batch, 1, h, w)
    return (y_p.reshape(batch, seq, d_model), y_s[:dec_batch].reshape(dec_batch, 1, d_model),
            p5(kf, h_fox, hd), p5(vf, h_fox, hd), logf_p.reshape(1, batch, seq, h_fox),
            p5(kd, h_diff, dd), p5(vd, h_diff, dd),
            s5(kfs, h_fox, hd), s5(vfs, h_fox, hd), logf_s.reshape(1, dec_batch, 1, h_fox),
            s5(kds, h_diff, dd), s5(vds, h_diff, dd))
```

```python
import functools
import math

import jax
import jax.numpy as jnp
from jax import lax
from jax.experimental import pallas as pl
from jax.experimental.pallas import tpu as pltpu

F32 = jnp.float32
BF16 = jnp.bfloat16
I32 = jnp.int32
U32 = jnp.uint32

LANES = 128
SUBLANES = 8
VMEM_LIMIT = 56 * 1024 * 1024

TOP_K = 4
SWIGLU_ALPHA = 1.702
SWIGLU_LIMIT = 7.0
LN_EPS = 1e-5
RMS_EPS = 1e-5
NEG_INF = -1e30

TOKEN_TILE = 256
ROW_TILE = 256
ATTN_TILE = 512
DECODE_PAGES = 8


def _params(sem, vmem=VMEM_LIMIT):
    return pltpu.CompilerParams(dimension_semantics=sem, vmem_limit_bytes=vmem)


def _log_sigmoid(x):
    return jnp.minimum(x, 0.0) - jnp.log1p(jnp.exp(-jnp.abs(x)))


def _dot(a, b):
    return jnp.dot(a, b, preferred_element_type=F32)


def _dot_nt(a, b):
    return lax.dot_general(a, b, (((1,), (1,)), ((), ())), preferred_element_type=F32)


def _dot_exact_rhs(x, rhs):
    hi = x.astype(BF16)
    r1 = x - hi.astype(F32)
    mid = r1.astype(BF16)
    lo = (r1 - mid.astype(F32)).astype(BF16)
    return _dot(hi, rhs) + _dot(mid, rhs) + _dot(lo, rhs)


def _layer_norm(z, g, b):
    mu = jnp.mean(z, axis=-1, keepdims=True)
    zc = z - mu
    var = jnp.mean(zc * zc, axis=-1, keepdims=True)
    return zc * lax.rsqrt(var + LN_EPS) * g + b


def _proj_kernel(x_ref, w_ref, wfg_ref, bfg_ref, qf, kf, vf, qd, kd, vd, lf, *, width, scale):
    xb = x_ref[...].astype(BF16)
    for s, o in enumerate((qf, kf, vf, qd, kd, vd)):
        r = _dot(xb, w_ref[:, s * width:(s + 1) * width])
        if o.dtype == BF16:
            r = r * scale
        o[...] = r.astype(o.dtype)
    g = _dot(xb, wfg_ref[...]) + bfg_ref[...]
    lf[...] = _log_sigmoid(g)


def _proj(x2d, w_main, w_fg, b_fg, *, tm, width, scale):
    n, d = x2d.shape
    const = lambda i: (0, 0)
    row = lambda i: (i, 0)
    wide = lambda dt: jax.ShapeDtypeStruct((n, width), dt)
    return pl.pallas_call(
        functools.partial(_proj_kernel, width=width, scale=scale),
        grid=(n // tm,),
        in_specs=[
            pl.BlockSpec((tm, d), row),
            pl.BlockSpec(w_main.shape, const, pipeline_mode=pl.Buffered(1)),
            pl.BlockSpec(w_fg.shape, const, pipeline_mode=pl.Buffered(1)),
            pl.BlockSpec(b_fg.shape, const, pipeline_mode=pl.Buffered(1)),
        ],
        out_specs=[pl.BlockSpec((tm, width), row)] * 6 + [pl.BlockSpec((tm, LANES), row)],
        out_shape=[wide(BF16), wide(F32), wide(F32), wide(BF16), wide(F32), wide(F32),
                   jax.ShapeDtypeStruct((n, LANES), F32)],
        compiler_params=_params(("arbitrary",)),
        name="proj",
    )(x2d, w_main, w_fg, b_fg)


def _cumsum_kernel(x_ref, o_ref):
    s = x_ref.shape[-1]
    r = lax.broadcasted_iota(I32, (LANES, LANES), 0)
    c = lax.broadcasted_iota(I32, (LANES, LANES), 1)
    upper = (r <= c).astype(BF16)
    carry = jnp.zeros((x_ref.shape[0], 1), F32)
    for j in range(s // LANES):
        sl = slice(j * LANES, (j + 1) * LANES)
        blk = _dot_exact_rhs(x_ref[:, sl], upper) + carry
        o_ref[:, sl] = blk
        carry = blk[:, LANES - 1:LANES]


def _cumsum(lf_t):
    b, h, s = lf_t.shape
    return pl.pallas_call(
        _cumsum_kernel,
        grid=(b,),
        in_specs=[pl.BlockSpec((None, h, s), lambda i: (i, 0, 0))],
        out_specs=pl.BlockSpec((None, h, s), lambda i: (i, 0, 0)),
        out_shape=jax.ShapeDtypeStruct((b, h, s), F32),
        compiler_params=_params(("arbitrary",)),
        name="cumsum",
    )(lf_t)


def _softmax_step(carry, s, v):
    m, l, acc = carry
    m_new = jnp.maximum(m, jnp.max(s, axis=-1, keepdims=True))
    alpha = jnp.exp(m - m_new)
    p = jnp.exp(s - m_new)
    l = alpha * l + jnp.sum(p, axis=-1, keepdims=True)
    acc = alpha * acc + _dot(p.astype(BF16), v)
    return m_new, l, acc


def _causal_mask(t):
    r = lax.broadcasted_iota(I32, (t, t), 0)
    c = lax.broadcasted_iota(I32, (t, t), 1)
    return r >= c


def _fox_attn_kernel(q_ref, k_ref, v_ref, c_ref, o_ref, kb, vb, *, t):
    qi = pl.program_id(2)

    @pl.when(qi == 0)
    def _():
        kb[...] = k_ref[...].astype(BF16)
        vb[...] = v_ref[...].astype(BF16)

    q = q_ref[...]
    c0 = c_ref[qi][:, 0:1]

    def tile(kj, carry, masked):
        k0 = pl.multiple_of(kj * t, t)
        s = _dot_nt(q, kb[pl.ds(k0, t), :]) + (c0 - c_ref[kj])
        if masked:
            s = jnp.where(_causal_mask(t), s, NEG_INF)
        return _softmax_step(carry, s, vb[pl.ds(k0, t), :])

    d = vb.shape[-1]
    init = (jnp.full((t, 1), NEG_INF, F32), jnp.zeros((t, 1), F32), jnp.zeros((t, d), F32))
    carry = lax.fori_loop(0, qi, lambda kj, c: tile(kj, c, False), init)
    _, l, acc = tile(qi, carry, True)
    o_ref[...] = (acc / l).astype(o_ref.dtype)


def _fox_attn(q, k, v, c_rows, *, batch, seq, heads, hd, t):
    nq = seq // t
    return pl.pallas_call(
        functools.partial(_fox_attn_kernel, t=t),
        grid=(batch, heads, nq),
        in_specs=[
            pl.BlockSpec((None, t, hd), lambda b, h, i: (b, i, h)),
            pl.BlockSpec((None, seq, hd), lambda b, h, i: (b, 0, h)),
            pl.BlockSpec((None, seq, hd), lambda b, h, i: (b, 0, h)),
            pl.BlockSpec((None, nq, 1, t), lambda b, h, i: (b * heads + h, 0, 0, 0)),
        ],
        out_specs=pl.BlockSpec((None, t, hd), lambda b, h, i: (b, i, h)),
        out_shape=jax.ShapeDtypeStruct((batch, seq, heads * hd), BF16),
        scratch_shapes=[pltpu.VMEM((seq, hd), BF16), pltpu.VMEM((seq, hd), BF16)],
        compiler_params=_params(("arbitrary", "arbitrary", "arbitrary")),
        name="fox_attn",
    )(q, k, v, c_rows)


def _diff_lambda(lam_ref, lam_init):
    lq1, lk1, lq2, lk2 = (lam_ref[i:i + 1, :] for i in range(4))
    e1 = jnp.exp(jnp.sum(lq1 * lk1, axis=-1, keepdims=True))
    e2 = jnp.exp(jnp.sum(lq2 * lk2, axis=-1, keepdims=True))
    return e1 - e2 + lam_init


def _sub_ln(o, g, lam_init):
    ms = jnp.mean(o * o, axis=-1, keepdims=True)
    return o * lax.rsqrt(ms + RMS_EPS) * g * (1.0 - lam_init)


def _diff_attn_kernel(slope_ref, q_ref, k_ref, v_ref, lam_ref, g_ref, o_ref, kb, vb, *, t, hd, lam_init):
    h = pl.program_id(1)
    qi = pl.program_id(2)

    @pl.when(qi == 0)
    def _():
        kb[...] = k_ref[...].astype(BF16)
        vb[...] = v_ref[...].astype(BF16)

    q = q_ref[...]
    q1, q2 = q[:, :hd], q[:, hd:]
    slope = slope_ref[h]
    col = lax.broadcasted_iota(I32, (1, t), 1)

    def tile(kj, carry, masked):
        c1, c2 = carry
        k0 = pl.multiple_of(kj * t, t)
        k = kb[pl.ds(k0, t), :]
        v = vb[pl.ds(k0, t), :]
        bias = slope * (col - (qi - kj) * t).astype(F32)
        s1 = _dot_nt(q1, k[:, :hd]) + bias
        s2 = _dot_nt(q2, k[:, hd:]) + bias
        if masked:
            mask = _causal_mask(t)
            s1 = jnp.where(mask, s1, NEG_INF)
            s2 = jnp.where(mask, s2, NEG_INF)
        return _softmax_step(c1, s1, v), _softmax_step(c2, s2, v)

    d = vb.shape[-1]
    one = (jnp.full((t, 1), NEG_INF, F32), jnp.zeros((t, 1), F32), jnp.zeros((t, d), F32))
    carry = lax.fori_loop(0, qi, lambda kj, c: tile(kj, c, False), (one, one))
    (_, l1, a1), (_, l2, a2) = tile(qi, carry, True)
    lam = _diff_lambda(lam_ref, lam_init)
    o = a1 / l1 - lam * (a2 / l2)
    o_ref[...] = _sub_ln(o, g_ref[...], lam_init).astype(o_ref.dtype)


def _diff_attn(slopes, q, k, v, lam_rows, g_subln, *, batch, seq, heads, hd, t, lam_init):
    nq = seq // t
    dd = 2 * hd
    return pl.pallas_call(
        functools.partial(_diff_attn_kernel, t=t, hd=hd, lam_init=lam_init),
        grid=(batch, heads, nq),
        in_specs=[
            pl.BlockSpec(memory_space=pltpu.SMEM),
            pl.BlockSpec((None, t, dd), lambda b, h, i: (b, i, h)),
            pl.BlockSpec((None, seq, dd), lambda b, h, i: (b, 0, h)),
            pl.BlockSpec((None, seq, dd), lambda b, h, i: (b, 0, h)),
            pl.BlockSpec(lam_rows.shape, lambda b, h, i: (0, 0)),
            pl.BlockSpec(g_subln.shape, lambda b, h, i: (0, 0)),
        ],
        out_specs=pl.BlockSpec((None, t, dd), lambda b, h, i: (b, i, h)),
        out_shape=jax.ShapeDtypeStruct((batch, seq, heads * dd), BF16),
        scratch_shapes=[pltpu.VMEM((seq, dd), BF16), pltpu.VMEM((seq, dd), BF16)],
        compiler_params=_params(("arbitrary", "arbitrary", "arbitrary")),
        name="diff_attn",
    )(slopes, q, k, v, lam_rows, g_subln)


def _decay_pre_kernel(x_ref, dec_ref, tot_ref):
    g, h, p = x_ref.shape
    x = x_ref[...].reshape(g * h, p)
    r = lax.broadcasted_iota(I32, (p, p), 0)
    c = lax.broadcasted_iota(I32, (p, p), 1)
    later = (r > c).astype(BF16)
    ones = jnp.ones((p, p), BF16)
    dec_ref[...] = _dot_exact_rhs(x, later).reshape(g, h, p)
    tot_ref[...] = _dot_exact_rhs(x, ones).reshape(g, h, p)


def _decay_pre(lf_pages, *, group):
    n, h, p = lf_pages.shape
    spec = pl.BlockSpec((group, h, p), lambda i: (i, 0, 0))
    shape = jax.ShapeDtypeStruct((n, h, p), F32)
    return pl.pallas_call(
        _decay_pre_kernel,
        grid=(n // group,),
        in_specs=[spec],
        out_specs=[spec, spec],
        out_shape=[shape, shape],
        compiler_params=_params(("arbitrary",)),
        name="decay_pre",
    )(lf_pages)


def _block_diag_rows(q, n):
    qt = jnp.concatenate([q] * n, axis=-1)
    row = lax.broadcasted_iota(I32, qt.shape, 0)
    blk = lax.broadcasted_iota(I32, qt.shape, 1) // LANES
    return jnp.where(row == blk, qt, jnp.zeros_like(qt))


def _decode_update(m_s, l_s, acc_s, scores, values):
    s = jnp.concatenate(scores, axis=-1)
    m_old = m_s[...]
    m_new = jnp.maximum(m_old, jnp.max(s, axis=-1, keepdims=True))
    alpha = jnp.exp(m_old - m_new)
    p = jnp.exp(s - m_new)
    l_s[...] = alpha * l_s[...] + jnp.sum(p, axis=-1, keepdims=True)
    page = scores[0].shape[-1]
    pv = _dot(p[:, :page].astype(BF16), values[0])
    for g in range(1, len(values)):
        pv = pv + _dot(p[:, g * page:(g + 1) * page].astype(BF16), values[g])
    acc_s[...] = alpha * acc_s[...] + pv
    m_s[...] = m_new


def _decode_init(m_s, l_s, acc_s):
    m_s[...] = jnp.full(m_s.shape, NEG_INF, F32)
    l_s[...] = jnp.zeros(l_s.shape, F32)
    acc_s[...] = jnp.zeros(acc_s.shape, F32)


def _round_bf16(x):
    return x.astype(BF16).astype(F32)


def _fox_decode_kernel(pt_ref, q_ref, kn_ref, vn_ref, lfn_ref, *refs, n_pages, heads):
    gp = n_pages
    k_refs, v_refs = refs[:gp], refs[gp:2 * gp]
    dec_refs, tot_refs = refs[2 * gp:3 * gp], refs[3 * gp:4 * gp]
    o_ref = refs[4 * gp]
    m_s, l_s, acc_s, run_s = refs[4 * gp + 1:]
    j = pl.program_id(1)

    @pl.when(j == 0)
    def _():
        _decode_init(m_s, l_s, acc_s)
        run_s[...] = jnp.broadcast_to(lfn_ref[...], run_s.shape)

    q = q_ref[...]
    qblk = _block_diag_rows(q, heads)
    run = run_s[...]
    scores, values = [], []
    for g in range(gp):
        kp = jnp.concatenate([k_refs[g][pl.ds(h, LANES, stride=heads), :] for h in range(heads)],
                             axis=-1).astype(BF16)
        vp = jnp.concatenate([v_refs[g][pl.ds(h, LANES, stride=heads), :] for h in range(heads)],
                             axis=-1).astype(BF16)
        scores.append(_dot_nt(qblk, kp) + dec_refs[g][...] + run)
        values.append(vp)
        run = run + tot_refs[g][...]
    run_s[...] = run
    _decode_update(m_s, l_s, acc_s, scores, values)

    @pl.when(j == pl.num_programs(1) - 1)
    def _():
        hd = q.shape[-1]
        acc = acc_s[...]
        row = lax.broadcasted_iota(I32, (heads, hd), 0)
        o = jnp.zeros((heads, hd), F32)
        for h in range(heads):
            o = o + jnp.where(row == h, acc[:, h * hd:(h + 1) * hd], 0.0)
        s_new = jnp.sum(q.astype(F32) * _round_bf16(kn_ref[...]), axis=-1, keepdims=True)
        m_old = m_s[...]
        m_new = jnp.maximum(m_old, s_new)
        alpha = jnp.exp(m_old - m_new)
        p_new = jnp.exp(s_new - m_new)
        l = alpha * l_s[...] + p_new
        o = alpha * o + _round_bf16(p_new) * _round_bf16(vn_ref[...])
        o_ref[...] = o / l


def _fox_decode(page_table, q, k_new, v_new, lf_new, cache_k, cache_v, dec, tot, *, heads, hd, page):
    nb, npg = page_table.shape
    gp = DECODE_PAGES
    rows = page * heads

    def kv_spec(g):
        return pl.BlockSpec((rows, hd), lambda b, j, pt: (pt[b, npg - 1 - (j * gp + g)], 0))

    def dec_spec(g):
        return pl.BlockSpec((None, heads, page), lambda b, j, pt: (pt[b, npg - 1 - (j * gp + g)], 0, 0))

    per_seq = lambda shape: pl.BlockSpec((None,) + shape, lambda b, j, pt: (b, 0, 0))
    return pl.pallas_call(
        functools.partial(_fox_decode_kernel, n_pages=gp, heads=heads),
        grid_spec=pltpu.PrefetchScalarGridSpec(
            num_scalar_prefetch=1,
            grid=(nb, npg // gp),
            in_specs=[per_seq((heads, hd)), per_seq((heads, hd)), per_seq((heads, hd)), per_seq((heads, 1))]
            + [kv_spec(g) for g in range(gp)] * 2 + [dec_spec(g) for g in range(gp)] * 2,
            out_specs=per_seq((heads, hd)),
            scratch_shapes=[pltpu.VMEM((heads, 1), F32), pltpu.VMEM((heads, 1), F32),
                            pltpu.VMEM((heads, heads * hd), F32), pltpu.VMEM((heads, page), F32)],
        ),
        out_shape=jax.ShapeDtypeStruct((nb, heads, hd), F32),
        compiler_params=_params(("arbitrary", "arbitrary")),
        name="fox_decode",
    )(page_table, q, k_new, v_new, lf_new, *([cache_k] * gp), *([cache_v] * gp), *([dec] * gp), *([tot] * gp))


def _diff_decode_kernel(pt_ref, q_ref, kn_ref, vn_ref, slope_ref, lam_ref, g_ref, *refs,
                        n_pages, heads, hd, past, lam_init):
    gp = n_pages
    k_refs, v_refs = refs[:gp], refs[gp:2 * gp]
    o_ref = refs[2 * gp]
    m_s, l_s, acc_s = refs[2 * gp + 1:]
    j = pl.program_id(1)
    nv = 2 * heads
    page = k_refs[0].shape[0]

    @pl.when(j == 0)
    def _():
        _decode_init(m_s, l_s, acc_s)

    q = q_ref[...]
    qblk = _block_diag_rows(q, nv)
    slope = slope_ref[...]
    lane = lax.broadcasted_iota(I32, (nv, page), 1)
    scores, values = [], []
    for g in range(gp):
        kp = jnp.concatenate([k_refs[g][:, h, :] for h in range(heads)], axis=-1).astype(BF16)
        vp = jnp.concatenate([v_refs[g][:, h, :] for h in range(heads)], axis=-1).astype(BF16)
        pos = (j * gp + g) * page + lane
        scores.append(_dot_nt(qblk, kp) - slope * (past - pos).astype(F32))
        values.append(vp)
    _decode_update(m_s, l_s, acc_s, scores, values)

    @pl.when(j == pl.num_programs(1) - 1)
    def _():
        dd = 2 * hd
        acc = acc_s[...]
        s_new = jnp.sum(q.astype(F32) * _round_bf16(kn_ref[...]), axis=-1, keepdims=True)
        m_old = m_s[...]
        m_new = jnp.maximum(m_old, s_new)
        alpha = jnp.exp(m_old - m_new)
        p_new = jnp.exp(s_new - m_new)
        l = alpha * l_s[...] + p_new
        pb = _round_bf16(p_new)
        vn = _round_bf16(vn_ref[...])
        lam = _diff_lambda(lam_ref, lam_init)
        outs = []
        for h in range(heads):
            blk = acc[:, h * dd:(h + 1) * dd]
            r1, r2 = 2 * h, 2 * h + 1
            o1 = (alpha[r1:r1 + 1] * blk[r1:r1 + 1] + pb[r1:r1 + 1] * vn[h:h + 1]) / l[r1:r1 + 1]
            o2 = (alpha[r2:r2 + 1] * blk[r2:r2 + 1] + pb[r2:r2 + 1] * vn[h:h + 1]) / l[r2:r2 + 1]
            outs.append(_sub_ln(o1 - lam * o2, g_ref[...], lam_init))
        o_ref[...] = jnp.concatenate(outs, axis=-1)


def _diff_decode(page_table, q, k_new, v_new, slope_rows, lam_rows, g_subln, cache_k, cache_v,
                 *, heads, hd, page, lam_init):
    nb, npg = page_table.shape
    gp = DECODE_PAGES
    dd = 2 * hd
    nv = 2 * heads

    def kv_spec(g):
        return pl.BlockSpec((None, page, heads, dd), lambda b, j, pt: (pt[b, j * gp + g], 0, 0, 0))

    per_seq = lambda shape: pl.BlockSpec((None,) + shape, lambda b, j, pt: (b, 0, 0))
    whole = lambda a: pl.BlockSpec(a.shape, lambda b, j, pt: (0, 0))
    return pl.pallas_call(
        functools.partial(_diff_decode_kernel, n_pages=gp, heads=heads, hd=hd,
                          past=npg * page, lam_init=lam_init),
        grid_spec=pltpu.PrefetchScalarGridSpec(
            num_scalar_prefetch=1,
            grid=(nb, npg // gp),
            in_specs=[per_seq((nv, hd)), per_seq((nv, hd)), per_seq((heads, dd)),
                      whole(slope_rows), whole(lam_rows), whole(g_subln)]
            + [kv_spec(g) for g in range(gp)] * 2,
            out_specs=per_seq((1, heads * dd)),
            scratch_shapes=[pltpu.VMEM((nv, 1), F32), pltpu.VMEM((nv, 1), F32),
                            pltpu.VMEM((nv, heads * dd), F32)],
        ),
        out_shape=jax.ShapeDtypeStruct((nb, 1, heads * dd), F32),
        compiler_params=_params(("arbitrary", "arbitrary")),
        name="diff_decode",
    )(page_table, q, k_new, v_new, slope_rows, lam_rows, g_subln, *([cache_k] * gp), *([cache_v] * gp))


def _pack_bf16_pairs(lo, hi):
    lo_bits = lax.bitcast_convert_type(_round_bf16(lo), U32) >> 16
    hi_bits = lax.bitcast_convert_type(_round_bf16(hi), U32) & jnp.uint32(0xFFFF0000)
    return lo_bits | hi_bits


def _unpack_bf16_pairs(w):
    lo = lax.bitcast_convert_type(w << 16, F32).astype(BF16)
    hi = lax.bitcast_convert_type(w & jnp.uint32(0xFFFF0000), F32).astype(BF16)
    return lo, hi


def _post_kernel(xp_ref, ofp_ref, odp_ref, pp_ref, xs_ref, ofs_ref, ods_ref, ps_ref,
                 wo_ref, wpg_ref, wpe_ref, wr_ref, br_ref, g1_ref, b1_ref,
                 h3_ref, base_ref, eidx_ref, wts_ref, cnt_ref, *, alpha, n_experts):
    i = pl.program_id(0)
    is_sample = i == pl.num_programs(0) - 1
    pick = lambda a, b: jnp.where(is_sample, a[...], b[...])
    x = pick(xs_ref, xp_ref)
    of = pick(ofs_ref, ofp_ref)
    od = pick(ods_ref, odp_ref)
    p = pick(ps_ref, pp_ref)
    tm, d = x.shape
    half = of.shape[-1]

    mix = _dot(of, wo_ref[:half, :]) + _dot(od, wo_ref[half:, :])
    h = _layer_norm(alpha * x + mix, g1_ref[...], b1_ref[...])
    hb = h.astype(BF16)
    ple = jax.nn.sigmoid(_dot(hb, wpg_ref[...])) * _dot(p.astype(BF16), wpe_ref[...])
    base_ref[...] = alpha * h + ple

    nw = d // (2 * LANES)
    for r in range(nw):
        lo = h[:, r * LANES:(r + 1) * LANES]
        hi = h[:, (r + nw) * LANES:(r + nw + 1) * LANES]
        h3_ref[pl.ds(r, tm, stride=nw), :] = _pack_bf16_pairs(lo, hi)

    logits = _dot(hb, wr_ref[...]) + br_ref[...]
    lane = lax.broadcasted_iota(I32, logits.shape, 1)
    lane_f = lane.astype(F32)
    work = logits
    vals, idxs = [], []
    for _ in range(TOP_K):
        mx = jnp.max(work, axis=-1, keepdims=True)
        ix = jnp.min(jnp.where(work == mx, lane_f, float(LANES)), axis=-1, keepdims=True)
        vals.append(mx)
        idxs.append(ix)
        work = jnp.where(lane_f == ix, -jnp.inf, work)
    exps = [jnp.exp(v - vals[0]) for v in vals]
    denom = exps[0]
    for e in exps[1:]:
        denom = denom + e
    eidx = jnp.zeros(logits.shape, F32)
    wts = jnp.zeros(logits.shape, F32)
    sel = jnp.zeros(logits.shape, F32)
    for k in range(TOP_K):
        eidx = jnp.where(lane == k, idxs[k], eidx)
        wts = jnp.where(lane == k, exps[k] / denom, wts)
        sel = sel + (lane_f == idxs[k]).astype(F32)
    eidx_ref[...] = eidx.astype(I32)
    wts_ref[...] = wts
    cnt_ref[...] = jnp.sum(sel, axis=0, keepdims=True)


def _post(prompt, sample, weights, *, n_tiles, tm, alpha, n_experts):
    xp, ofp, odp, pp = prompt
    xs, ofs, ods, ps = sample
    last = n_tiles - 2
    pmap = lambda i: (jnp.minimum(i, last), 0)
    smap = lambda i: (0, 0)
    const = lambda i: (0, 0)
    d = xp.shape[-1]
    n_tot = n_tiles * tm
    nw = d // (2 * LANES)
    in_specs = [pl.BlockSpec((tm, a.shape[-1]), pmap) for a in prompt]
    in_specs += [pl.BlockSpec((tm, a.shape[-1]), smap) for a in sample]
    in_specs += [pl.BlockSpec(w.shape, const, pipeline_mode=pl.Buffered(1)) for w in weights]
    return pl.pallas_call(
        functools.partial(_post_kernel, alpha=alpha, n_experts=n_experts),
        grid=(n_tiles,),
        in_specs=in_specs,
        out_specs=[
            pl.BlockSpec((tm * nw, LANES), lambda i: (i, 0)),
            pl.BlockSpec((tm, d), lambda i: (i, 0)),
            pl.BlockSpec((tm, LANES), lambda i: (i, 0)),
            pl.BlockSpec((tm, LANES), lambda i: (i, 0)),
            pl.BlockSpec((None, 1, LANES), lambda i: (i, 0, 0)),
        ],
        out_shape=[
            jax.ShapeDtypeStruct((n_tot * nw, LANES), U32),
            jax.ShapeDtypeStruct((n_tot, d), F32),
            jax.ShapeDtypeStruct((n_tot, LANES), I32),
            jax.ShapeDtypeStruct((n_tot, LANES), F32),
            jax.ShapeDtypeStruct((n_tiles, 1, LANES), F32),
        ],
        compiler_params=_params(("arbitrary",)),
        name="post",
    )(*prompt, *sample, *weights)


def _rank_kernel(eidx_ref, start_ref, dest_ref):
    eidx = eidx_ref[...]
    tm = eidx.shape[0]
    lane = lax.broadcasted_iota(I32, eidx.shape, 1)
    picks = [lane == eidx[:, k:k + 1] for k in range(TOP_K)]
    sel = picks[0]
    for pk in picks[1:]:
        sel = sel | pk
    r = lax.broadcasted_iota(I32, (tm, tm), 0)
    c = lax.broadcasted_iota(I32, (tm, tm), 1)
    before = (r > c).astype(BF16)
    pos = _dot(before, sel.astype(BF16)) + start_ref[...]
    dest = jnp.zeros(eidx.shape, F32)
    for k in range(TOP_K):
        dk = jnp.sum(jnp.where(picks[k], pos, 0.0), axis=-1, keepdims=True)
        dest = jnp.where(lane == k, dk, dest)
    dest_ref[...] = dest.astype(I32)


def _rank(eidx, start, *, tm):
    n_tot = eidx.shape[0]
    return pl.pallas_call(
        _rank_kernel,
        grid=(n_tot // tm,),
        in_specs=[pl.BlockSpec((tm, LANES), lambda i: (i, 0)),
                  pl.BlockSpec((None, 1, LANES), lambda i: (i, 0, 0))],
        out_specs=pl.BlockSpec((tm, LANES), lambda i: (i, 0)),
        out_shape=jax.ShapeDtypeStruct((n_tot, LANES), I32),
        compiler_params=_params(("arbitrary",)),
        name="rank",
    )(eidx, start)


def _dispatch_kernel(tail_ref, pad_ref, nu_ref, dest_ref, h3_ref, xs_ref, idx_s, zero_s, sem_idx, sem_zero, sem,
                     *, tm, nw, rows, n_experts, n_blocks):
    i = pl.program_id(0)
    blk = rows * nw

    @pl.when(i == 0)
    def _():
        zero_s[...] = jnp.zeros(zero_s.shape, zero_s.dtype)
        pieces = [1 << b for b in range(rows.bit_length() - 1)]

        def tail_copy(e, piece):
            done = pad_ref[e] & ~(2 * piece - 1)
            first = pl.multiple_of((tail_ref[e] + done) * nw, nw)
            return pltpu.make_async_copy(zero_s.at[pl.ds(0, piece * nw), :],
                                         xs_ref.at[pl.ds(first, piece * nw), :], sem_zero)

        def block_copy(b):
            return pltpu.make_async_copy(
                zero_s, xs_ref.at[pl.ds(pl.multiple_of(b * blk, blk), blk), :], sem_zero)

        def tails(e, wait):
            for piece in pieces:
                @pl.when((pad_ref[e] & piece) != 0)
                def _():
                    cp = tail_copy(e, piece)
                    cp.wait() if wait else cp.start()

        def start_tail(e, c):
            tails(e, False)
            return c

        def wait_tail(e, c):
            tails(e, True)
            return c

        def start_block(b, c):
            block_copy(b).start()
            return c

        def wait_block(b, c):
            block_copy(b).wait()
            return c

        lax.fori_loop(0, n_experts, start_tail, 0)
        lax.fori_loop(nu_ref[0], n_blocks, start_block, 0)
        lax.fori_loop(0, n_experts, wait_tail, 0)
        lax.fori_loop(nu_ref[0], n_blocks, wait_block, 0)

    cp = pltpu.make_async_copy(dest_ref, idx_s, sem_idx)
    cp.start()
    cp.wait()

    def row_copy(t, k):
        src = h3_ref.at[pl.ds(pl.multiple_of(t * nw, nw), nw), :]
        dst = xs_ref.at[pl.ds(pl.multiple_of(idx_s[t * TOP_K + k] * nw, nw), nw), :]
        return pltpu.make_async_copy(src, dst, sem)

    def issue(t, carry):
        for k in range(TOP_K):
            row_copy(t, k).start()
        return carry

    def drain(t, carry):
        for k in range(TOP_K):
            row_copy(t, k).wait()
        return carry

    lax.fori_loop(0, tm, issue, 0, unroll=8)
    lax.fori_loop(0, tm, drain, 0, unroll=8)


def _dispatch(tail, pad, n_used, dest_flat, h3, *, tm, nw, rows, n_blocks):
    n_tiles = dest_flat.shape[0] // (tm * TOP_K)
    n_experts = tail.shape[0]
    return pl.pallas_call(
        functools.partial(_dispatch_kernel, tm=tm, nw=nw, rows=rows, n_experts=n_experts, n_blocks=n_blocks),
        grid_spec=pltpu.PrefetchScalarGridSpec(
            num_scalar_prefetch=3,
            grid=(n_tiles,),
            in_specs=[pl.BlockSpec((tm * TOP_K,), lambda i, tl, pd, nu: (i,)),
                      pl.BlockSpec((tm * nw, LANES), lambda i, tl, pd, nu: (i, 0))],
            out_specs=pl.BlockSpec(memory_space=pl.ANY),
            scratch_shapes=[pltpu.SMEM((tm * TOP_K,), I32), pltpu.VMEM((rows * nw, LANES), U32),
                            pltpu.SemaphoreType.DMA(()), pltpu.SemaphoreType.DMA(()),
                            pltpu.SemaphoreType.DMA(())],
        ),
        out_shape=jax.ShapeDtypeStruct((n_blocks * rows * nw, LANES), U32),
        compiler_params=_params(("arbitrary",)),
        name="dispatch",
    )(tail, pad, n_used, dest_flat, h3)


PAIR = 2 * LANES


def _first_block_of_expert(be_ref, r):
    return (r == 0) | (be_ref[r] != be_ref[jnp.maximum(r - 1, 0)])


def _mlp1_kernel(be_ref, nu_ref, x_ref, w_ref, b_ref, o_ref, wp_ref, *, rows, nw):
    r = pl.program_id(1)
    active = r < nu_ref[0]
    n_groups = w_ref.shape[-1] // PAIR

    @pl.when(active & _first_block_of_expert(be_ref, r))
    def _():
        k = lax.broadcasted_iota(I32, (PAIR, PAIR), 0)
        n = lax.broadcasted_iota(I32, (PAIR, PAIR), 1)
        src = jnp.where(n < LANES, 2 * n, 2 * (n - LANES) + 1)
        perm = (k == src).astype(BF16)
        for g in range(n_groups):
            sl = slice(g * PAIR, (g + 1) * PAIR)
            wp_ref[:, sl] = _dot(w_ref[:, sl].astype(BF16), perm).astype(BF16)

    @pl.when(active)
    def _():
        cols = [None] * (2 * nw)
        for w in range(nw):
            cols[w], cols[w + nw] = _unpack_bf16_pairs(x_ref[pl.ds(w, rows, stride=nw), :])
        xb = jnp.concatenate(cols, axis=-1)
        u = _dot(xb, wp_ref[...]) + b_ref[...]
        outs = []
        for g in range(n_groups):
            glu = jnp.minimum(u[:, g * PAIR:g * PAIR + LANES], SWIGLU_LIMIT)
            lin = jnp.clip(u[:, g * PAIR + LANES:(g + 1) * PAIR], -SWIGLU_LIMIT, SWIGLU_LIMIT)
            outs.append(glu * jax.nn.sigmoid(SWIGLU_ALPHA * glu) * (lin + 1.0))
        o_ref[...] = jnp.concatenate(outs, axis=-1).astype(o_ref.dtype)

    @pl.when(jnp.logical_not(active))
    def _():
        o_ref[...] = jnp.zeros(o_ref.shape, o_ref.dtype)


def _mlp1(block_expert, n_used, xs, w1, b1_grouped, *, rows, nw, tn):
    n_blocks = block_expert.shape[0]
    e, d, two_dff = w1.shape
    tf = tn // 2
    return pl.pallas_call(
        functools.partial(_mlp1_kernel, rows=rows, nw=nw),
        grid_spec=pltpu.PrefetchScalarGridSpec(
            num_scalar_prefetch=2,
            grid=(two_dff // tn, n_blocks),
            in_specs=[pl.BlockSpec((rows * nw, LANES), lambda j, r, be, nu: (jnp.minimum(r, nu[0] - 1), 0)),
                      pl.BlockSpec((None, d, tn), lambda j, r, be, nu: (be[r], 0, j)),
                      pl.BlockSpec((None, 1, tn), lambda j, r, be, nu: (be[r], 0, j))],
            out_specs=pl.BlockSpec((rows, tf), lambda j, r, be, nu: (r, j)),
            scratch_shapes=[pltpu.VMEM((d, tn), BF16)],
        ),
        out_shape=jax.ShapeDtypeStruct((n_blocks * rows, two_dff // 2), BF16),
        compiler_params=_params(("arbitrary", "arbitrary")),
        name="mlp1",
    )(block_expert, n_used, xs, w1, b1_grouped)


def _mlp2_kernel(be_ref, nu_ref, a_ref, w_ref, b_ref, y3_ref, wb_ref, *, rows):
    r = pl.program_id(0)
    active = r < nu_ref[0]

    @pl.when(active & _first_block_of_expert(be_ref, r))
    def _():
        wb_ref[...] = w_ref[...].astype(BF16)

    @pl.when(active)
    def _():
        y = _dot(a_ref[...], wb_ref[...]) + b_ref[...]
        nc = y.shape[-1] // LANES
        for c in range(nc):
            y3_ref[pl.ds(c, rows, stride=nc), :] = y[:, c * LANES:(c + 1) * LANES]

    @pl.when(jnp.logical_not(active))
    def _():
        y3_ref[...] = jnp.zeros(y3_ref.shape, y3_ref.dtype)


def _mlp2(block_expert, n_used, act, w2, b2, *, rows):
    n_blocks = block_expert.shape[0]
    e, dff, d = w2.shape
    nc = d // LANES
    return pl.pallas_call(
        functools.partial(_mlp2_kernel, rows=rows),
        grid_spec=pltpu.PrefetchScalarGridSpec(
            num_scalar_prefetch=2,
            grid=(n_blocks,),
            in_specs=[pl.BlockSpec((rows, dff), lambda r, be, nu: (jnp.minimum(r, nu[0] - 1), 0)),
                      pl.BlockSpec((None, dff, d), lambda r, be, nu: (be[r], 0, 0)),
                      pl.BlockSpec((None, 1, d), lambda r, be, nu: (be[r], 0, 0))],
            out_specs=pl.BlockSpec((rows * nc, LANES), lambda r, be, nu: (r, 0)),
            scratch_shapes=[pltpu.VMEM((dff, d), BF16)],
        ),
        out_shape=jax.ShapeDtypeStruct((n_blocks * rows * nc, LANES), F32),
        compiler_params=_params(("arbitrary",)),
        name="mlp2",
    )(block_expert, n_used, act, w2, b2)


def _combine_kernel(dest_ref, wts_ref, base_ref, g_ref, b_ref, y3_ref, yp_ref, ys_ref,
                    idx_s, ybuf, sem_idx, sem, *, tm, nc):
    i = pl.program_id(0)
    cp = pltpu.make_async_copy(dest_ref, idx_s, sem_idx)
    cp.start()
    cp.wait()

    def row_copy(t, k):
        src = y3_ref.at[pl.ds(pl.multiple_of(idx_s[t * TOP_K + k] * nc, nc), nc), :]
        dst = ybuf.at[pl.ds(pl.multiple_of((k * tm + t) * nc, nc), nc), :]
        return pltpu.make_async_copy(src, dst, sem)

    def issue(t, carry):
        for k in range(TOP_K):
            row_copy(t, k).start()
        return carry

    def drain(t, carry):
        for k in range(TOP_K):
            row_copy(t, k).wait()
        return carry

    lax.fori_loop(0, tm, issue, 0, unroll=8)
    lax.fori_loop(0, tm, drain, 0, unroll=8)

    wts = wts_ref[...]
    wk = [jnp.broadcast_to(wts[:, k:k + 1], (tm, LANES)) for k in range(TOP_K)]
    chunks = []
    for c in range(nc):
        acc = wk[0] * ybuf[pl.ds(c, tm, stride=nc), :]
        for k in range(1, TOP_K):
            acc = acc + wk[k] * ybuf[pl.ds(k * tm * nc + c, tm, stride=nc), :]
        chunks.append(acc)
    moe = jnp.concatenate(chunks, axis=-1)
    out = _layer_norm(base_ref[...] + moe, g_ref[...], b_ref[...])
    is_sample = i == pl.num_programs(0) - 1

    @pl.when(jnp.logical_not(is_sample))
    def _():
        yp_ref[...] = out

    @pl.when(is_sample)
    def _():
        ys_ref[...] = out


def _combine(dest_flat, wts, base, ln_g, ln_b, y3, *, tm, n_prompt):
    n_tot, d = base.shape
    n_tiles = n_tot // tm
    nc = d // LANES
    last = n_tiles - 2
    const = lambda i: (0, 0)
    return pl.pallas_call(
        functools.partial(_combine_kernel, tm=tm, nc=nc),
        grid=(n_tiles,),
        in_specs=[pl.BlockSpec((tm * TOP_K,), lambda i: (i,)),
                  pl.BlockSpec((tm, LANES), lambda i: (i, 0)),
                  pl.BlockSpec((tm, d), lambda i: (i, 0)),
                  pl.BlockSpec(ln_g.shape, const),
                  pl.BlockSpec(ln_b.shape, const),
                  pl.BlockSpec(memory_space=pl.ANY)],
        out_specs=[pl.BlockSpec((tm, d), lambda i: (jnp.minimum(i, last), 0)),
                   pl.BlockSpec((tm, d), const)],
        out_shape=[jax.ShapeDtypeStruct((n_prompt, d), F32), jax.ShapeDtypeStruct((tm, d), F32)],
        scratch_shapes=[pltpu.SMEM((tm * TOP_K,), I32), pltpu.VMEM((TOP_K * tm * nc, LANES), F32),
                        pltpu.SemaphoreType.DMA(()), pltpu.SemaphoreType.DMA(())],
        compiler_params=_params(("arbitrary",)),
        name="combine",
    )(dest_flat, wts, base, ln_g, ln_b, y3)


def kernel(x_prompt, x_sample, cache_fox_k, cache_fox_v, cache_fox_logf, cache_diff_k, cache_diff_v, page_table, p_prompt, p_sample, w_in, b_fgate, lambda_q1, lambda_k1, lambda_q2, lambda_k2, g_subln, w_o, ln1_g, ln1_b, w_router, b_router, w_mlp1, b_mlp1, w_mlp2, b_mlp2, w_pe, w_pg, ln2_g, ln2_b):
    depth = w_in.shape[0]
    assert depth == 1, "single-layer step only"
    batch, seq, d_model = x_prompt.shape
    dec_batch, dec_seq, _ = x_sample.shape
    assert dec_seq == 1
    _, n_pool, page, h_fox, hd = cache_fox_k.shape
    _, _, _, h_diff, dd = cache_diff_k.shape
    assert hd == LANES and dd == 2 * hd and h_fox == SUBLANES and 2 * h_diff == SUBLANES
    fox_w, diff_w = h_fox * hd, h_diff * dd
    assert fox_w == diff_w
    n_experts = w_router.shape[-1]
    d_ff = w_mlp2.shape[2]
    n_prompt = batch * seq
    lam_init = 0.8 - 0.6 * math.exp(-0.3 * 0)
    deep_alpha = (2 * depth) ** 0.25
    scale = hd ** -0.5
    tm = TOKEN_TILE

    wi = w_in[0]
    o3, o4 = 3 * fox_w, 3 * fox_w + h_fox
    w_main = jnp.concatenate([wi[:, :o3], wi[:, o4:]], axis=1).astype(BF16)
    w_fg = jnp.pad(wi[:, o3:o4], ((0, 0), (0, LANES - h_fox))).astype(BF16)
    b_fg = jnp.pad(b_fgate[0], (0, LANES - h_fox)).reshape(1, LANES)
    lam_rows = jnp.stack([lambda_q1[0], lambda_k1[0], lambda_q2[0], lambda_k2[0]])
    g_sub = g_subln[0].reshape(1, dd)
    slopes = 2.0 ** (-8.0 * jnp.arange(1, h_diff + 1, dtype=F32) / h_diff)

    xp2 = x_prompt.reshape(n_prompt, d_model)
    qf, kf, vf, qd, kd, vd, lf = _proj(xp2, w_main, w_fg, b_fg, tm=tm, width=fox_w, scale=scale)
    logf_p = lf[:, :h_fox]
    lf_t = jnp.swapaxes(logf_p.reshape(batch, seq, h_fox), 1, 2)
    t = ATTN_TILE
    c_rows = _cumsum(lf_t).reshape(batch * h_fox, seq // t, 1, t)
    b3 = lambda a: a.reshape(batch, seq, a.shape[-1])
    o_f = _fox_attn(b3(qf), b3(kf), b3(vf), c_rows, batch=batch, seq=seq, heads=h_fox, hd=hd, t=t)
    o_d = _diff_attn(slopes, b3(qd), b3(kd), b3(vd), lam_rows, g_sub,
                     batch=batch, seq=seq, heads=h_diff, hd=hd, t=t, lam_init=lam_init)

    xs2 = x_sample.reshape(dec_batch, d_model)
    qfs, kfs, vfs, qds, kds, vds, lfs = _proj(xs2, w_main, w_fg, b_fg, tm=dec_batch, width=fox_w, scale=scale)
    logf_s = lfs[:, :h_fox]
    lf_pages = jnp.swapaxes(cache_fox_logf[0], 1, 2)
    dec, tot = _decay_pre(lf_pages, group=math.gcd(n_pool, 64))
    o_fs = _fox_decode(page_table, qfs.reshape(dec_batch, h_fox, hd), kfs.reshape(dec_batch, h_fox, hd),
                       vfs.reshape(dec_batch, h_fox, hd), logf_s.reshape(dec_batch, h_fox, 1),
                       cache_fox_k.reshape(n_pool * page * h_fox, hd),
                       cache_fox_v.reshape(n_pool * page * h_fox, hd), dec, tot,
                       heads=h_fox, hd=hd, page=page)
    slope_rows = jnp.broadcast_to(jnp.repeat(slopes, 2)[:, None], (2 * h_diff, page))
    o_ds = _diff_decode(page_table, qds.reshape(dec_batch, 2 * h_diff, hd), kds.reshape(dec_batch, 2 * h_diff, hd),
                        vds.reshape(dec_batch, h_diff, dd), slope_rows, lam_rows, g_sub,
                        cache_diff_k.reshape(n_pool, page, h_diff, dd),
                        cache_diff_v.reshape(n_pool, page, h_diff, dd),
                        heads=h_diff, hd=hd, page=page, lam_init=lam_init)

    pad_rows = lambda a: jnp.pad(a, ((0, tm - a.shape[0]), (0, 0)))
    prompt_in = (xp2, o_f.reshape(n_prompt, fox_w), o_d.reshape(n_prompt, diff_w),
                 p_prompt[0].reshape(n_prompt, -1))
    sample_in = (pad_rows(xs2), pad_rows(o_fs.reshape(dec_batch, fox_w).astype(BF16)),
                 pad_rows(o_ds.reshape(dec_batch, diff_w).astype(BF16)),
                 pad_rows(p_sample[0].reshape(dec_batch, -1)))
    w_r = jnp.pad(w_router[0], ((0, 0), (0, LANES - n_experts))).astype(BF16)
    b_r = jnp.pad(b_router[0], (0, LANES - n_experts), constant_values=NEG_INF).reshape(1, LANES)
    weights = (w_o[0].astype(BF16), w_pg[0].astype(BF16), w_pe[0].astype(BF16), w_r, b_r,
               ln1_g[0].reshape(1, d_model), ln1_b[0].reshape(1, d_model))
    n_tiles = n_prompt // tm + 1
    h3, base, eidx, wts, cnt = _post(prompt_in, sample_in, weights, n_tiles=n_tiles, tm=tm,
                                     alpha=deep_alpha, n_experts=n_experts)

    rows = ROW_TILE
    cnt = cnt[:, 0, :].astype(I32)
    totals = jnp.sum(cnt, axis=0)
    padded = ((totals + rows - 1) // rows) * rows
    group_end = jnp.cumsum(padded)
    group_off = group_end - padded
    start = group_off[None, :] + jnp.cumsum(cnt, axis=0) - cnt
    n_tot = n_tiles * tm
    n_blocks = (n_tot * TOP_K) // rows + n_experts
    n_used = (group_end[n_experts - 1] // rows).astype(I32).reshape(1)
    blk = jnp.minimum(jnp.arange(n_blocks, dtype=I32), n_used[0] - 1)
    block_expert = jnp.sum((blk * rows)[:, None] >= group_end[None, :n_experts], axis=1).astype(I32)
    tail = (group_off + totals)[:n_experts].astype(I32)
    dest = _rank(eidx, start.astype(F32).reshape(n_tiles, 1, LANES), tm=tm)
    dest_flat = dest[:, :TOP_K].reshape(n_tot * TOP_K)

    nw = d_model // (2 * LANES)
    pad = (padded - totals)[:n_experts].astype(I32)
    xs = _dispatch(tail, pad, n_used, dest_flat, h3, tm=tm, nw=nw, rows=rows, n_blocks=n_blocks)
    b1 = b_mlp1[0]
    b1_grouped = b1.reshape(n_experts, -1, LANES, 2).swapaxes(2, 3).reshape(n_experts, 1, b1.shape[-1])
    act = _mlp1(block_expert, n_used, xs, w_mlp1[0], b1_grouped, rows=rows, nw=nw, tn=2 * 1024)
    y3 = _mlp2(block_expert, n_used, act, w_mlp2[0], b_mlp2[0][:, None, :], rows=rows)
    y_p, y_s = _combine(dest_flat, wts, base, ln2_g[0].reshape(1, d_model), ln2_b[0].reshape(1, d_model), y3,
                        tm=tm, n_prompt=n_prompt)

    p5 = lambda a, h, w: a.reshape(1, batch, seq, h, w)
    s5 = lambda a, h, w: a.reshape(1, dec_batch, 1, h, w)
    return (y_p.reshape(batch, seq, d_model), y_s[:dec_batch].reshape(dec_batch, 1, d_model),
            p5(kf, h_fox, hd), p5(vf, h_fox, hd), logf_p.reshape(1, batch, seq, h_fox),
            p5(kd, h_diff, dd), p5(vd, h_diff, dd),
            s5(kfs, h_fox, hd), s5(vfs, h_fox, hd), logf_s.reshape(1, dec_batch, 1, h_fox),
            s5(kds, h_diff, dd), s5(vds, h_diff, dd))
```

```python
import functools
import math

import jax
import jax.numpy as jnp
from jax import lax
from jax.experimental import pallas as pl
from jax.experimental.pallas import tpu as pltpu

F32 = jnp.float32
BF16 = jnp.bfloat16
I32 = jnp.int32
U32 = jnp.uint32

LANES = 128
SUBLANES = 8
VMEM_LIMIT = 56 * 1024 * 1024

TOP_K = 4
SWIGLU_ALPHA = 1.702
SWIGLU_LIMIT = 7.0
LN_EPS = 1e-5
RMS_EPS = 1e-5
NEG_INF = -1e30
LOG2E = math.log2(math.e)

TOKEN_TILE = 256
ROW_TILE = 256
ATTN_TILE = 512
DECODE_PAGES = 8


def _params(sem, vmem=VMEM_LIMIT):
    return pltpu.CompilerParams(dimension_semantics=sem, vmem_limit_bytes=vmem)


def _log_sigmoid(x):
    return jnp.minimum(x, 0.0) - jnp.log1p(jnp.exp(-jnp.abs(x)))


def _dot(a, b):
    return jnp.dot(a, b, preferred_element_type=F32)


def _dot_nt(a, b):
    return lax.dot_general(a, b, (((1,), (1,)), ((), ())), preferred_element_type=F32)


def _dot_exact_rhs(x, rhs):
    hi = x.astype(BF16)
    r1 = x - hi.astype(F32)
    mid = r1.astype(BF16)
    lo = (r1 - mid.astype(F32)).astype(BF16)
    return _dot(hi, rhs) + _dot(mid, rhs) + _dot(lo, rhs)


def _layer_norm(z, g, b):
    mu = jnp.mean(z, axis=-1, keepdims=True)
    zc = z - mu
    var = jnp.mean(zc * zc, axis=-1, keepdims=True)
    return zc * lax.rsqrt(var + LN_EPS) * g + b


def _proj_kernel(x_ref, w_ref, wfg_ref, bfg_ref, qf, kf, vf, qd, kd, vd, lf, *, width, scale):
    xb = x_ref[...].astype(BF16)
    for s, o in enumerate((qf, kf, vf, qd, kd, vd)):
        r = _dot(xb, w_ref[:, s * width:(s + 1) * width])
        if o.dtype == BF16:
            r = r * scale
        o[...] = r.astype(o.dtype)
    g = _dot(xb, wfg_ref[...]) + bfg_ref[...]
    lf[...] = _log_sigmoid(g)


def _proj(x2d, w_main, w_fg, b_fg, *, tm, width, scale):
    n, d = x2d.shape
    const = lambda i: (0, 0)
    row = lambda i: (i, 0)
    wide = lambda dt: jax.ShapeDtypeStruct((n, width), dt)
    return pl.pallas_call(
        functools.partial(_proj_kernel, width=width, scale=scale),
        grid=(n // tm,),
        in_specs=[
            pl.BlockSpec((tm, d), row),
            pl.BlockSpec(w_main.shape, const, pipeline_mode=pl.Buffered(1)),
            pl.BlockSpec(w_fg.shape, const, pipeline_mode=pl.Buffered(1)),
            pl.BlockSpec(b_fg.shape, const, pipeline_mode=pl.Buffered(1)),
        ],
        out_specs=[pl.BlockSpec((tm, width), row)] * 6 + [pl.BlockSpec((tm, LANES), row)],
        out_shape=[wide(BF16), wide(F32), wide(F32), wide(BF16), wide(F32), wide(F32),
                   jax.ShapeDtypeStruct((n, LANES), F32)],
        compiler_params=_params(("arbitrary",)),
        name="proj",
    )(x2d, w_main, w_fg, b_fg)


def _cumsum_kernel(x_ref, o_ref):
    s = x_ref.shape[-1]
    r = lax.broadcasted_iota(I32, (LANES, LANES), 0)
    c = lax.broadcasted_iota(I32, (LANES, LANES), 1)
    upper = (r <= c).astype(BF16)
    carry = jnp.zeros((x_ref.shape[0], 1), F32)
    for j in range(s // LANES):
        sl = slice(j * LANES, (j + 1) * LANES)
        blk = _dot_exact_rhs(x_ref[:, sl], upper) + carry
        o_ref[:, sl] = blk
        carry = blk[:, LANES - 1:LANES]


def _cumsum(lf_t):
    b, h, s = lf_t.shape
    return pl.pallas_call(
        _cumsum_kernel,
        grid=(b,),
        in_specs=[pl.BlockSpec((None, h, s), lambda i: (i, 0, 0))],
        out_specs=pl.BlockSpec((None, h, s), lambda i: (i, 0, 0)),
        out_shape=jax.ShapeDtypeStruct((b, h, s), F32),
        compiler_params=_params(("arbitrary",)),
        name="cumsum",
    )(lf_t)


def _softmax_step(carry, s, v):
    m, l, acc = carry
    m_new = jnp.maximum(m, jnp.max(s, axis=-1, keepdims=True))
    alpha = jnp.exp2(m - m_new)
    p = jnp.exp2(s - m_new)
    l = alpha * l + jnp.sum(p, axis=-1, keepdims=True)
    acc = alpha * acc + _dot(p.astype(BF16), v)
    return m_new, l, acc


def _causal_mask(t):
    r = lax.broadcasted_iota(I32, (t, t), 0)
    c = lax.broadcasted_iota(I32, (t, t), 1)
    return r >= c


FOX_GROUP = 2


def _fox_attn_kernel(q_ref, k_ref, v_ref, c_ref, o_ref, kb, vb, *, t, hd):
    qi = pl.program_id(2)

    @pl.when(qi == 0)
    def _():
        kb[...] = k_ref[...].astype(BF16)
        vb[...] = v_ref[...].astype(BF16)

    q = q_ref[...]
    heads = [slice(u * hd, (u + 1) * hd) for u in range(FOX_GROUP)]
    c0 = [c_ref[u, qi][:, 0:1] for u in range(FOX_GROUP)]

    def tile(kj, carry, masked):
        k0 = pl.multiple_of(kj * t, t)
        k = kb[pl.ds(k0, t), :]
        v = vb[pl.ds(k0, t), :]
        out = []
        for u, sl in enumerate(heads):
            s = _dot_nt(q[:, sl], k[:, sl]) + LOG2E * (c0[u] - c_ref[u, kj])
            if masked:
                s = jnp.where(_causal_mask(t), s, NEG_INF)
            out.append(_softmax_step(carry[u], s, v[:, sl]))
        return tuple(out)

    one = (jnp.full((t, 1), NEG_INF, F32), jnp.zeros((t, 1), F32), jnp.zeros((t, hd), F32))
    carry = lax.fori_loop(0, qi, lambda kj, c: tile(kj, c, False), (one,) * FOX_GROUP)
    final = tile(qi, carry, True)
    o_ref[...] = jnp.concatenate([acc / l for _, l, acc in final], axis=-1).astype(o_ref.dtype)


def _fox_attn(q, k, v, c_rows, *, batch, seq, heads, hd, t):
    nq = seq // t
    gw = FOX_GROUP * hd
    c_rows = c_rows.reshape(batch * heads // FOX_GROUP, FOX_GROUP, nq, 1, t)
    return pl.pallas_call(
        functools.partial(_fox_attn_kernel, t=t, hd=hd),
        grid=(batch, heads // FOX_GROUP, nq),
        in_specs=[
            pl.BlockSpec((None, t, gw), lambda b, h, i: (b, i, h)),
            pl.BlockSpec((None, seq, gw), lambda b, h, i: (b, 0, h)),
            pl.BlockSpec((None, seq, gw), lambda b, h, i: (b, 0, h)),
            pl.BlockSpec((None, FOX_GROUP, nq, 1, t),
                         lambda b, h, i: (b * (heads // FOX_GROUP) + h, 0, 0, 0, 0)),
        ],
        out_specs=pl.BlockSpec((None, t, gw), lambda b, h, i: (b, i, h)),
        out_shape=jax.ShapeDtypeStruct((batch, seq, heads * hd), BF16),
        scratch_shapes=[pltpu.VMEM((seq, gw), BF16), pltpu.VMEM((seq, gw), BF16)],
        compiler_params=_params(("arbitrary", "arbitrary", "arbitrary")),
        name="fox_attn",
    )(q, k, v, c_rows)


def _diff_lambda(lam_ref, lam_init):
    lq1, lk1, lq2, lk2 = (lam_ref[i:i + 1, :] for i in range(4))
    e1 = jnp.exp(jnp.sum(lq1 * lk1, axis=-1, keepdims=True))
    e2 = jnp.exp(jnp.sum(lq2 * lk2, axis=-1, keepdims=True))
    return e1 - e2 + lam_init


def _sub_ln(o, g, lam_init):
    ms = jnp.mean(o * o, axis=-1, keepdims=True)
    return o * lax.rsqrt(ms + RMS_EPS) * g * (1.0 - lam_init)


def _diff_attn_kernel(slope_ref, q_ref, k_ref, v_ref, lam_ref, g_ref, o_ref, kb, vb, *, t, hd, lam_init):
    h = pl.program_id(1)
    qi = pl.program_id(2)

    @pl.when(qi == 0)
    def _():
        kb[...] = k_ref[...].astype(BF16)
        vb[...] = v_ref[...].astype(BF16)

    q = q_ref[...]
    q1, q2 = q[:, :hd], q[:, hd:]
    slope = slope_ref[h] * LOG2E
    col = lax.broadcasted_iota(I32, (1, t), 1)

    def tile(kj, carry, masked):
        c1, c2 = carry
        k0 = pl.multiple_of(kj * t, t)
        k = kb[pl.ds(k0, t), :]
        v = vb[pl.ds(k0, t), :]
        bias = slope * (col - (qi - kj) * t).astype(F32)
        s1 = _dot_nt(q1, k[:, :hd]) + bias
        s2 = _dot_nt(q2, k[:, hd:]) + bias
        if masked:
            mask = _causal_mask(t)
            s1 = jnp.where(mask, s1, NEG_INF)
            s2 = jnp.where(mask, s2, NEG_INF)
        return _softmax_step(c1, s1, v), _softmax_step(c2, s2, v)

    d = vb.shape[-1]
    one = (jnp.full((t, 1), NEG_INF, F32), jnp.zeros((t, 1), F32), jnp.zeros((t, d), F32))
    carry = lax.fori_loop(0, qi, lambda kj, c: tile(kj, c, False), (one, one))
    (_, l1, a1), (_, l2, a2) = tile(qi, carry, True)
    lam = _diff_lambda(lam_ref, lam_init)
    o = a1 / l1 - lam * (a2 / l2)
    o_ref[...] = _sub_ln(o, g_ref[...], lam_init).astype(o_ref.dtype)


def _diff_attn(slopes, q, k, v, lam_rows, g_subln, *, batch, seq, heads, hd, t, lam_init):
    nq = seq // t
    dd = 2 * hd
    return pl.pallas_call(
        functools.partial(_diff_attn_kernel, t=t, hd=hd, lam_init=lam_init),
        grid=(batch, heads, nq),
        in_specs=[
            pl.BlockSpec(memory_space=pltpu.SMEM),
            pl.BlockSpec((None, t, dd), lambda b, h, i: (b, i, h)),
            pl.BlockSpec((None, seq, dd), lambda b, h, i: (b, 0, h)),
            pl.BlockSpec((None, seq, dd), lambda b, h, i: (b, 0, h)),
            pl.BlockSpec(lam_rows.shape, lambda b, h, i: (0, 0)),
            pl.BlockSpec(g_subln.shape, lambda b, h, i: (0, 0)),
        ],
        out_specs=pl.BlockSpec((None, t, dd), lambda b, h, i: (b, i, h)),
        out_shape=jax.ShapeDtypeStruct((batch, seq, heads * dd), BF16),
        scratch_shapes=[pltpu.VMEM((seq, dd), BF16), pltpu.VMEM((seq, dd), BF16)],
        compiler_params=_params(("arbitrary", "arbitrary", "arbitrary")),
        name="diff_attn",
    )(slopes, q, k, v, lam_rows, g_subln)


def _decay_pre_kernel(x_ref, dec_ref, tot_ref):
    g, h, p = x_ref.shape
    x = x_ref[...].reshape(g * h, p)
    r = lax.broadcasted_iota(I32, (p, p), 0)
    c = lax.broadcasted_iota(I32, (p, p), 1)
    later = (r > c).astype(BF16)
    ones = jnp.ones((p, p), BF16)
    dec_ref[...] = _dot_exact_rhs(x, later).reshape(g, h, p)
    tot_ref[...] = _dot_exact_rhs(x, ones).reshape(g, h, p)


def _decay_pre(lf_pages, *, group):
    n, h, p = lf_pages.shape
    spec = pl.BlockSpec((group, h, p), lambda i: (i, 0, 0))
    shape = jax.ShapeDtypeStruct((n, h, p), F32)
    return pl.pallas_call(
        _decay_pre_kernel,
        grid=(n // group,),
        in_specs=[spec],
        out_specs=[spec, spec],
        out_shape=[shape, shape],
        compiler_params=_params(("arbitrary",)),
        name="decay_pre",
    )(lf_pages)


def _block_diag_rows(q, n):
    qt = jnp.concatenate([q] * n, axis=-1)
    row = lax.broadcasted_iota(I32, qt.shape, 0)
    blk = lax.broadcasted_iota(I32, qt.shape, 1) // LANES
    return jnp.where(row == blk, qt, jnp.zeros_like(qt))


def _decode_update(m_s, l_s, acc_s, scores, values):
    s = jnp.concatenate(scores, axis=-1)
    m_old = m_s[...]
    m_new = jnp.maximum(m_old, jnp.max(s, axis=-1, keepdims=True))
    alpha = jnp.exp2(m_old - m_new)
    p = jnp.exp2(s - m_new)
    l_s[...] = alpha * l_s[...] + jnp.sum(p, axis=-1, keepdims=True)
    page = scores[0].shape[-1]
    pv = _dot(p[:, :page].astype(BF16), values[0])
    for g in range(1, len(values)):
        pv = pv + _dot(p[:, g * page:(g + 1) * page].astype(BF16), values[g])
    acc_s[...] = alpha * acc_s[...] + pv
    m_s[...] = m_new


def _decode_init(m_s, l_s, acc_s):
    m_s[...] = jnp.full(m_s.shape, NEG_INF, F32)
    l_s[...] = jnp.zeros(l_s.shape, F32)
    acc_s[...] = jnp.zeros(acc_s.shape, F32)


def _round_bf16(x):
    return x.astype(BF16).astype(F32)


def _fox_decode_kernel(pt_ref, q_ref, kn_ref, vn_ref, lfn_ref, *refs, n_pages, heads):
    gp = n_pages
    k_refs, v_refs = refs[:gp], refs[gp:2 * gp]
    dec_refs, tot_refs = refs[2 * gp:3 * gp], refs[3 * gp:4 * gp]
    o_ref = refs[4 * gp]
    m_s, l_s, acc_s, run_s = refs[4 * gp + 1:]
    j = pl.program_id(1)

    @pl.when(j == 0)
    def _():
        _decode_init(m_s, l_s, acc_s)
        run_s[...] = jnp.broadcast_to(lfn_ref[...], run_s.shape)

    q = q_ref[...]
    qblk = _block_diag_rows(q, heads)
    run = run_s[...]
    scores, values = [], []
    for g in range(gp):
        kp = jnp.concatenate([k_refs[g][pl.ds(h, LANES, stride=heads), :] for h in range(heads)],
                             axis=-1).astype(BF16)
        vp = jnp.concatenate([v_refs[g][pl.ds(h, LANES, stride=heads), :] for h in range(heads)],
                             axis=-1).astype(BF16)
        scores.append(_dot_nt(qblk, kp) + LOG2E * (dec_refs[g][...] + run))
        values.append(vp)
        run = run + tot_refs[g][...]
    run_s[...] = run
    _decode_update(m_s, l_s, acc_s, scores, values)

    @pl.when(j == pl.num_programs(1) - 1)
    def _():
        hd = q.shape[-1]
        acc = acc_s[...]
        row = lax.broadcasted_iota(I32, (heads, hd), 0)
        o = jnp.zeros((heads, hd), F32)
        for h in range(heads):
            o = o + jnp.where(row == h, acc[:, h * hd:(h + 1) * hd], 0.0)
        s_new = jnp.sum(q.astype(F32) * _round_bf16(kn_ref[...]), axis=-1, keepdims=True)
        m_old = m_s[...]
        m_new = jnp.maximum(m_old, s_new)
        alpha = jnp.exp2(m_old - m_new)
        p_new = jnp.exp2(s_new - m_new)
        l = alpha * l_s[...] + p_new
        o = alpha * o + _round_bf16(p_new) * _round_bf16(vn_ref[...])
        o_ref[...] = o / l


def _fox_decode(page_table, q, k_new, v_new, lf_new, cache_k, cache_v, dec, tot, *, heads, hd, page):
    nb, npg = page_table.shape
    gp = DECODE_PAGES
    rows = page * heads

    def kv_spec(g):
        return pl.BlockSpec((rows, hd), lambda b, j, pt: (pt[b, npg - 1 - (j * gp + g)], 0))

    def dec_spec(g):
        return pl.BlockSpec((None, heads, page), lambda b, j, pt: (pt[b, npg - 1 - (j * gp + g)], 0, 0))

    per_seq = lambda shape: pl.BlockSpec((None,) + shape, lambda b, j, pt: (b, 0, 0))
    return pl.pallas_call(
        functools.partial(_fox_decode_kernel, n_pages=gp, heads=heads),
        grid_spec=pltpu.PrefetchScalarGridSpec(
            num_scalar_prefetch=1,
            grid=(nb, npg // gp),
            in_specs=[per_seq((heads, hd)), per_seq((heads, hd)), per_seq((heads, hd)), per_seq((heads, 1))]
            + [kv_spec(g) for g in range(gp)] * 2 + [dec_spec(g) for g in range(gp)] * 2,
            out_specs=per_seq((heads, hd)),
            scratch_shapes=[pltpu.VMEM((heads, 1), F32), pltpu.VMEM((heads, 1), F32),
                            pltpu.VMEM((heads, heads * hd), F32), pltpu.VMEM((heads, page), F32)],
        ),
        out_shape=jax.ShapeDtypeStruct((nb, heads, hd), F32),
        compiler_params=_params(("arbitrary", "arbitrary")),
        name="fox_decode",
    )(page_table, q, k_new, v_new, lf_new, *([cache_k] * gp), *([cache_v] * gp), *([dec] * gp), *([tot] * gp))


def _diff_decode_kernel(pt_ref, q_ref, kn_ref, vn_ref, slope_ref, lam_ref, g_ref, *refs,
                        n_pages, heads, hd, past, lam_init):
    gp = n_pages
    k_refs, v_refs = refs[:gp], refs[gp:2 * gp]
    o_ref = refs[2 * gp]
    m_s, l_s, acc_s = refs[2 * gp + 1:]
    j = pl.program_id(1)
    nv = 2 * heads
    dd = 2 * hd
    page = k_refs[0].shape[0]
    cols = page * heads

    @pl.when(j == 0)
    def _():
        _decode_init(m_s, l_s, acc_s)

    q = q_ref[...]
    qq = jnp.concatenate([q, q], axis=-1)
    row = lax.broadcasted_iota(I32, (nv, dd), 0)
    half = lax.broadcasted_iota(I32, (nv, dd), 1) // hd
    qsel = jnp.where(half == row % 2, qq, jnp.zeros_like(qq))
    r = lax.broadcasted_iota(I32, (nv, cols), 0)
    c = lax.broadcasted_iota(I32, (nv, cols), 1)
    own = (c % heads) == (r // 2)
    tok = c // heads
    slope = slope_ref[...] * LOG2E
    scores, values = [], []
    for g in range(gp):
        kr = k_refs[g][...].reshape(cols, dd).astype(BF16)
        vr = v_refs[g][...].reshape(cols, dd).astype(BF16)
        pos = (j * gp + g) * page + tok
        s = _dot_nt(qsel, kr) - slope * (past - pos).astype(F32)
        scores.append(jnp.where(own, s, NEG_INF))
        values.append(vr)
    _decode_update(m_s, l_s, acc_s, scores, values)

    @pl.when(j == pl.num_programs(1) - 1)
    def _():
        s_new = jnp.sum(q.astype(F32) * _round_bf16(kn_ref[...]), axis=-1, keepdims=True)
        m_old = m_s[...]
        m_new = jnp.maximum(m_old, s_new)
        alpha = jnp.exp2(m_old - m_new)
        p_new = jnp.exp2(s_new - m_new)
        l = alpha * l_s[...] + p_new
        o = (alpha * acc_s[...] + _round_bf16(p_new) * _round_bf16(vn_ref[...])) / l
        lam = _diff_lambda(lam_ref, lam_init)
        outs = [_sub_ln(o[2 * h:2 * h + 1] - lam * o[2 * h + 1:2 * h + 2], g_ref[...], lam_init)
                for h in range(heads)]
        o_ref[...] = jnp.concatenate(outs, axis=-1)


def _diff_decode(page_table, q, k_new, v_new, slope_rows, lam_rows, g_subln, cache_k, cache_v,
                 *, heads, hd, page, lam_init):
    nb, npg = page_table.shape
    gp = DECODE_PAGES
    dd = 2 * hd
    nv = 2 * heads

    def kv_spec(g):
        return pl.BlockSpec((None, page, heads, dd), lambda b, j, pt: (pt[b, j * gp + g], 0, 0, 0))

    per_seq = lambda shape: pl.BlockSpec((None,) + shape, lambda b, j, pt: (b, 0, 0))
    whole = lambda a: pl.BlockSpec(a.shape, lambda b, j, pt: (0, 0))
    return pl.pallas_call(
        functools.partial(_diff_decode_kernel, n_pages=gp, heads=heads, hd=hd,
                          past=npg * page, lam_init=lam_init),
        grid_spec=pltpu.PrefetchScalarGridSpec(
            num_scalar_prefetch=1,
            grid=(nb, npg // gp),
            in_specs=[per_seq((nv, hd)), per_seq((nv, hd)), per_seq((nv, dd)),
                      whole(slope_rows), whole(lam_rows), whole(g_subln)]
            + [kv_spec(g) for g in range(gp)] * 2,
            out_specs=per_seq((1, heads * dd)),
            scratch_shapes=[pltpu.VMEM((nv, 1), F32), pltpu.VMEM((nv, 1), F32),
                            pltpu.VMEM((nv, dd), F32)],
        ),
        out_shape=jax.ShapeDtypeStruct((nb, 1, heads * dd), F32),
        compiler_params=_params(("arbitrary", "arbitrary")),
        name="diff_decode",
    )(page_table, q, k_new, v_new, slope_rows, lam_rows, g_subln, *([cache_k] * gp), *([cache_v] * gp))


def _post_kernel(xp_ref, ofp_ref, odp_ref, pp_ref, xs_ref, ofs_ref, ods_ref, ps_ref,
                 wo_ref, wpg_ref, wpe_ref, wr_ref, br_ref, g1_ref, b1_ref,
                 h3_ref, base_ref, eidx_ref, wts_ref, cnt_ref, *, alpha, n_experts):
    i = pl.program_id(0)
    is_sample = i == pl.num_programs(0) - 1
    pick = lambda a, b: jnp.where(is_sample, a[...], b[...])
    x = pick(xs_ref, xp_ref)
    of = pick(ofs_ref, ofp_ref)
    od = pick(ods_ref, odp_ref)
    p = pick(ps_ref, pp_ref)
    tm, d = x.shape
    half = of.shape[-1]

    mix = _dot(of, wo_ref[:half, :]) + _dot(od, wo_ref[half:, :])
    h = _layer_norm(alpha * x + mix, g1_ref[...], b1_ref[...])
    hb = h.astype(BF16)
    ple = jax.nn.sigmoid(_dot(hb, wpg_ref[...])) * _dot(p.astype(BF16), wpe_ref[...])
    base_ref[...] = alpha * h + ple

    nw = d // LANES
    for r in range(nw):
        h3_ref[pl.ds(r, tm, stride=nw), :] = h[:, r * LANES:(r + 1) * LANES]

    logits = _dot(hb, wr_ref[...]) + br_ref[...]
    lane = lax.broadcasted_iota(I32, logits.shape, 1)
    lane_f = lane.astype(F32)
    work = logits
    vals, idxs = [], []
    for _ in range(TOP_K):
        mx = jnp.max(work, axis=-1, keepdims=True)
        ix = jnp.min(jnp.where(work == mx, lane_f, float(LANES)), axis=-1, keepdims=True)
        vals.append(mx)
        idxs.append(ix)
        work = jnp.where(lane_f == ix, -jnp.inf, work)
    exps = [jnp.exp(v - vals[0]) for v in vals]
    denom = exps[0]
    for e in exps[1:]:
        denom = denom + e
    eidx = jnp.zeros(logits.shape, F32)
    wts = jnp.zeros(logits.shape, F32)
    sel = jnp.zeros(logits.shape, F32)
    for k in range(TOP_K):
        eidx = jnp.where(lane == k, idxs[k], eidx)
        wts = jnp.where(lane == k, exps[k] / denom, wts)
        sel = sel + (lane_f == idxs[k]).astype(F32)
    eidx_ref[...] = eidx.astype(I32)
    wts_ref[...] = wts
    cnt_ref[...] = jnp.sum(sel, axis=0, keepdims=True)


def _post(prompt, sample, weights, *, n_tiles, tm, alpha, n_experts):
    xp, ofp, odp, pp = prompt
    xs, ofs, ods, ps = sample
    last = n_tiles - 2
    pmap = lambda i: (jnp.minimum(i, last), 0)
    smap = lambda i: (0, 0)
    const = lambda i: (0, 0)
    d = xp.shape[-1]
    n_tot = n_tiles * tm
    nw = d // LANES
    in_specs = [pl.BlockSpec((tm, a.shape[-1]), pmap) for a in prompt]
    in_specs += [pl.BlockSpec((tm, a.shape[-1]), smap) for a in sample]
    in_specs += [pl.BlockSpec(w.shape, const, pipeline_mode=pl.Buffered(1)) for w in weights]
    return pl.pallas_call(
        functools.partial(_post_kernel, alpha=alpha, n_experts=n_experts),
        grid=(n_tiles,),
        in_specs=in_specs,
        out_specs=[
            pl.BlockSpec((tm * nw, LANES), lambda i: (i, 0)),
            pl.BlockSpec((tm, d), lambda i: (i, 0)),
            pl.BlockSpec((tm, LANES), lambda i: (i, 0)),
            pl.BlockSpec((tm, LANES), lambda i: (i, 0)),
            pl.BlockSpec((None, 1, LANES), lambda i: (i, 0, 0)),
        ],
        out_shape=[
            jax.ShapeDtypeStruct((n_tot * nw, LANES), F32),
            jax.ShapeDtypeStruct((n_tot, d), F32),
            jax.ShapeDtypeStruct((n_tot, LANES), I32),
            jax.ShapeDtypeStruct((n_tot, LANES), F32),
            jax.ShapeDtypeStruct((n_tiles, 1, LANES), F32),
        ],
        compiler_params=_params(("arbitrary",)),
        name="post",
    )(*prompt, *sample, *weights)


def _rank_kernel(eidx_ref, start_ref, dest_ref):
    eidx = eidx_ref[...]
    tm = eidx.shape[0]
    lane = lax.broadcasted_iota(I32, eidx.shape, 1)
    picks = [lane == eidx[:, k:k + 1] for k in range(TOP_K)]
    sel = picks[0]
    for pk in picks[1:]:
        sel = sel | pk
    r = lax.broadcasted_iota(I32, (tm, tm), 0)
    c = lax.broadcasted_iota(I32, (tm, tm), 1)
    before = (r > c).astype(BF16)
    pos = _dot(before, sel.astype(BF16)) + start_ref[...]
    dest = jnp.zeros(eidx.shape, F32)
    for k in range(TOP_K):
        dk = jnp.sum(jnp.where(picks[k], pos, 0.0), axis=-1, keepdims=True)
        dest = jnp.where(lane == k, dk, dest)
    dest_ref[...] = dest.astype(I32)


def _rank(eidx, start, *, tm):
    n_tot = eidx.shape[0]
    return pl.pallas_call(
        _rank_kernel,
        grid=(n_tot // tm,),
        in_specs=[pl.BlockSpec((tm, LANES), lambda i: (i, 0)),
                  pl.BlockSpec((None, 1, LANES), lambda i: (i, 0, 0))],
        out_specs=pl.BlockSpec((tm, LANES), lambda i: (i, 0)),
        out_shape=jax.ShapeDtypeStruct((n_tot, LANES), I32),
        compiler_params=_params(("arbitrary",)),
        name="rank",
    )(eidx, start)


def _dispatch_kernel(tail_ref, pad_ref, nu_ref, dest_ref, h3_ref, xs_ref, idx_s, zero_s, sem_idx, sem_zero, sem,
                     *, tm, nw, rows, n_experts, n_blocks):
    i = pl.program_id(0)
    blk = rows * nw

    @pl.when(i == 0)
    def _():
        zero_s[...] = jnp.zeros(zero_s.shape, zero_s.dtype)
        pieces = [1 << b for b in range(rows.bit_length() - 1)]

        def tail_copy(e, piece):
            done = pad_ref[e] & ~(2 * piece - 1)
            first = pl.multiple_of((tail_ref[e] + done) * nw, nw)
            return pltpu.make_async_copy(zero_s.at[pl.ds(0, piece * nw), :],
                                         xs_ref.at[pl.ds(first, piece * nw), :], sem_zero)

        def block_copy(b):
            return pltpu.make_async_copy(
                zero_s, xs_ref.at[pl.ds(pl.multiple_of(b * blk, blk), blk), :], sem_zero)

        def tails(e, wait):
            for piece in pieces:
                @pl.when((pad_ref[e] & piece) != 0)
                def _():
                    cp = tail_copy(e, piece)
                    cp.wait() if wait else cp.start()

        def start_tail(e, c):
            tails(e, False)
            return c

        def wait_tail(e, c):
            tails(e, True)
            return c

        def start_block(b, c):
            block_copy(b).start()
            return c

        def wait_block(b, c):
            block_copy(b).wait()
            return c

        lax.fori_loop(0, n_experts, start_tail, 0)
        lax.fori_loop(nu_ref[0], n_blocks, start_block, 0)
        lax.fori_loop(0, n_experts, wait_tail, 0)
        lax.fori_loop(nu_ref[0], n_blocks, wait_block, 0)

    cp = pltpu.make_async_copy(dest_ref, idx_s, sem_idx)
    cp.start()
    cp.wait()

    def row_copy(t, k):
        src = h3_ref.at[pl.ds(pl.multiple_of(t * nw, nw), nw), :]
        dst = xs_ref.at[pl.ds(pl.multiple_of(idx_s[t * TOP_K + k] * nw, nw), nw), :]
        return pltpu.make_async_copy(src, dst, sem)

    def issue(t, carry):
        for k in range(TOP_K):
            row_copy(t, k).start(priority=k % 2)
        return carry

    def drain(t, carry):
        for k in range(TOP_K):
            row_copy(t, k).wait()
        return carry

    lax.fori_loop(0, tm, issue, 0, unroll=8)
    lax.fori_loop(0, tm, drain, 0, unroll=8)


def _dispatch(tail, pad, n_used, dest_flat, h3, *, tm, nw, rows, n_blocks):
    n_tiles = dest_flat.shape[0] // (tm * TOP_K)
    n_experts = tail.shape[0]
    return pl.pallas_call(
        functools.partial(_dispatch_kernel, tm=tm, nw=nw, rows=rows, n_experts=n_experts, n_blocks=n_blocks),
        grid_spec=pltpu.PrefetchScalarGridSpec(
            num_scalar_prefetch=3,
            grid=(n_tiles,),
            in_specs=[pl.BlockSpec((tm * TOP_K,), lambda i, tl, pd, nu: (i,)),
                      pl.BlockSpec((tm * nw, LANES), lambda i, tl, pd, nu: (i, 0))],
            out_specs=pl.BlockSpec(memory_space=pl.ANY),
            scratch_shapes=[pltpu.SMEM((tm * TOP_K,), I32), pltpu.VMEM((rows * nw, LANES), F32),
                            pltpu.SemaphoreType.DMA(()), pltpu.SemaphoreType.DMA(()),
                            pltpu.SemaphoreType.DMA(())],
        ),
        out_shape=jax.ShapeDtypeStruct((n_blocks * rows * nw, LANES), F32),
        compiler_params=_params(("arbitrary",)),
        name="dispatch",
    )(tail, pad, n_used, dest_flat, h3)


PAIR = 2 * LANES


def _first_block_of_expert(be_ref, r):
    return (r == 0) | (be_ref[r] != be_ref[jnp.maximum(r - 1, 0)])


def _mlp1_kernel(be_ref, nu_ref, x_ref, w_ref, b_ref, o_ref, wp_ref, *, rows, nw):
    r = pl.program_id(1)
    active = r < nu_ref[0]
    n_groups = w_ref.shape[-1] // PAIR

    @pl.when(active & _first_block_of_expert(be_ref, r))
    def _():
        k = lax.broadcasted_iota(I32, (PAIR, PAIR), 0)
        n = lax.broadcasted_iota(I32, (PAIR, PAIR), 1)
        src = jnp.where(n < LANES, 2 * n, 2 * (n - LANES) + 1)
        perm = (k == src).astype(BF16)
        for g in range(n_groups):
            sl = slice(g * PAIR, (g + 1) * PAIR)
            wp_ref[:, sl] = _dot(w_ref[:, sl].astype(BF16), perm).astype(BF16)

    @pl.when(active)
    def _():
        xb = jnp.concatenate([x_ref[pl.ds(w, rows, stride=nw), :].astype(BF16) for w in range(nw)], axis=-1)
        u = _dot(xb, wp_ref[...]) + b_ref[...]
        outs = []
        for g in range(n_groups):
            glu = jnp.minimum(u[:, g * PAIR:g * PAIR + LANES], SWIGLU_LIMIT)
            lin = jnp.clip(u[:, g * PAIR + LANES:(g + 1) * PAIR], -SWIGLU_LIMIT, SWIGLU_LIMIT)
            outs.append(glu * jax.nn.sigmoid(SWIGLU_ALPHA * glu) * (lin + 1.0))
        o_ref[...] = jnp.concatenate(outs, axis=-1).astype(o_ref.dtype)

    @pl.when(jnp.logical_not(active))
    def _():
        o_ref[...] = jnp.zeros(o_ref.shape, o_ref.dtype)


def _mlp1(block_expert, n_used, xs, w1, b1_grouped, *, rows, nw, tn):
    n_blocks = block_expert.shape[0]
    e, d, two_dff = w1.shape
    tf = tn // 2
    return pl.pallas_call(
        functools.partial(_mlp1_kernel, rows=rows, nw=nw),
        grid_spec=pltpu.PrefetchScalarGridSpec(
            num_scalar_prefetch=2,
            grid=(two_dff // tn, n_blocks),
            in_specs=[pl.BlockSpec((rows * nw, LANES), lambda j, r, be, nu: (jnp.minimum(r, nu[0] - 1), 0)),
                      pl.BlockSpec((None, d, tn), lambda j, r, be, nu: (be[r], 0, j)),
                      pl.BlockSpec((None, 1, tn), lambda j, r, be, nu: (be[r], 0, j))],
            out_specs=pl.BlockSpec((rows, tf), lambda j, r, be, nu: (r, j)),
            scratch_shapes=[pltpu.VMEM((d, tn), BF16)],
        ),
        out_shape=jax.ShapeDtypeStruct((n_blocks * rows, two_dff // 2), BF16),
        compiler_params=_params(("arbitrary", "arbitrary")),
        name="mlp1",
    )(block_expert, n_used, xs, w1, b1_grouped)


def _mlp2_kernel(be_ref, nu_ref, a_ref, w_ref, b_ref, y3_ref, wb_ref, *, rows):
    r = pl.program_id(0)
    active = r < nu_ref[0]

    @pl.when(active & _first_block_of_expert(be_ref, r))
    def _():
        wb_ref[...] = w_ref[...].astype(BF16)

    @pl.when(active)
    def _():
        y = _dot(a_ref[...], wb_ref[...]) + b_ref[...]
        nc = y.shape[-1] // LANES
        for c in range(nc):
            y3_ref[pl.ds(c, rows, stride=nc), :] = y[:, c * LANES:(c + 1) * LANES]

    @pl.when(jnp.logical_not(active))
    def _():
        y3_ref[...] = jnp.zeros(y3_ref.shape, y3_ref.dtype)


def _mlp2(block_expert, n_used, act, w2, b2, *, rows):
    n_blocks = block_expert.shape[0]
    e, dff, d = w2.shape
    nc = d // LANES
    return pl.pallas_call(
        functools.partial(_mlp2_kernel, rows=rows),
        grid_spec=pltpu.PrefetchScalarGridSpec(
            num_scalar_prefetch=2,
            grid=(n_blocks,),
            in_specs=[pl.BlockSpec((rows, dff), lambda r, be, nu: (jnp.minimum(r, nu[0] - 1), 0)),
                      pl.BlockSpec((None, dff, d), lambda r, be, nu: (be[r], 0, 0)),
                      pl.BlockSpec((None, 1, d), lambda r, be, nu: (be[r], 0, 0))],
            out_specs=pl.BlockSpec((rows * nc, LANES), lambda r, be, nu: (r, 0)),
            scratch_shapes=[pltpu.VMEM((dff, d), BF16)],
        ),
        out_shape=jax.ShapeDtypeStruct((n_blocks * rows * nc, LANES), F32),
        compiler_params=_params(("arbitrary",)),
        name="mlp2",
    )(block_expert, n_used, act, w2, b2)


def _combine_kernel(dest_ref, wts_ref, base_ref, g_ref, b_ref, y3_ref, yp_ref, ys_ref,
                    idx_s, ybuf, sem_idx, sem, *, tm, nc):
    i = pl.program_id(0)
    cp = pltpu.make_async_copy(dest_ref, idx_s, sem_idx)
    cp.start()
    cp.wait()

    def row_copy(t, k):
        src = y3_ref.at[pl.ds(pl.multiple_of(idx_s[t * TOP_K + k] * nc, nc), nc), :]
        dst = ybuf.at[pl.ds(pl.multiple_of((k * tm + t) * nc, nc), nc), :]
        return pltpu.make_async_copy(src, dst, sem)

    def issue(t, carry):
        for k in range(TOP_K):
            row_copy(t, k).start(priority=k % 2)
        return carry

    def drain(t, carry):
        for k in range(TOP_K):
            row_copy(t, k).wait()
        return carry

    lax.fori_loop(0, tm, issue, 0, unroll=8)
    lax.fori_loop(0, tm, drain, 0, unroll=8)

    wts = wts_ref[...]
    wk = [jnp.broadcast_to(wts[:, k:k + 1], (tm, LANES)) for k in range(TOP_K)]
    chunks = []
    for c in range(nc):
        acc = wk[0] * ybuf[pl.ds(c, tm, stride=nc), :]
        for k in range(1, TOP_K):
            acc = acc + wk[k] * ybuf[pl.ds(k * tm * nc + c, tm, stride=nc), :]
        chunks.append(acc)
    moe = jnp.concatenate(chunks, axis=-1)
    out = _layer_norm(base_ref[...] + moe, g_ref[...], b_ref[...])
    is_sample = i == pl.num_programs(0) - 1

    @pl.when(jnp.logical_not(is_sample))
    def _():
        yp_ref[...] = out

    @pl.when(is_sample)
    def _():
        ys_ref[...] = out


def _combine(dest_flat, wts, base, ln_g, ln_b, y3, *, tm, n_prompt):
    n_tot, d = base.shape
    n_tiles = n_tot // tm
    nc = d // LANES
    last = n_tiles - 2
    const = lambda i: (0, 0)
    return pl.pallas_call(
        functools.partial(_combine_kernel, tm=tm, nc=nc),
        grid=(n_tiles,),
        in_specs=[pl.BlockSpec((tm * TOP_K,), lambda i: (i,)),
                  pl.BlockSpec((tm, LANES), lambda i: (i, 0)),
                  pl.BlockSpec((tm, d), lambda i: (i, 0)),
                  pl.BlockSpec(ln_g.shape, const),
                  pl.BlockSpec(ln_b.shape, const),
                  pl.BlockSpec(memory_space=pl.ANY)],
        out_specs=[pl.BlockSpec((tm, d), lambda i: (jnp.minimum(i, last), 0)),
                   pl.BlockSpec((tm, d), const)],
        out_shape=[jax.ShapeDtypeStruct((n_prompt, d), F32), jax.ShapeDtypeStruct((tm, d), F32)],
        scratch_shapes=[pltpu.SMEM((tm * TOP_K,), I32), pltpu.VMEM((TOP_K * tm * nc, LANES), F32),
                        pltpu.SemaphoreType.DMA(()), pltpu.SemaphoreType.DMA(())],
        compiler_params=_params(("arbitrary",)),
        name="combine",
    )(dest_flat, wts, base, ln_g, ln_b, y3)


def kernel(x_prompt, x_sample, cache_fox_k, cache_fox_v, cache_fox_logf, cache_diff_k, cache_diff_v, page_table, p_prompt, p_sample, w_in, b_fgate, lambda_q1, lambda_k1, lambda_q2, lambda_k2, g_subln, w_o, ln1_g, ln1_b, w_router, b_router, w_mlp1, b_mlp1, w_mlp2, b_mlp2, w_pe, w_pg, ln2_g, ln2_b):
    depth = w_in.shape[0]
    assert depth == 1, "single-layer step only"
    batch, seq, d_model = x_prompt.shape
    dec_batch, dec_seq, _ = x_sample.shape
    assert dec_seq == 1
    _, n_pool, page, h_fox, hd = cache_fox_k.shape
    _, _, _, h_diff, dd = cache_diff_k.shape
    assert hd == LANES and dd == 2 * hd and h_fox == SUBLANES and 2 * h_diff == SUBLANES
    fox_w, diff_w = h_fox * hd, h_diff * dd
    assert fox_w == diff_w
    n_experts = w_router.shape[-1]
    d_ff = w_mlp2.shape[2]
    n_prompt = batch * seq
    lam_init = 0.8 - 0.6 * math.exp(-0.3 * 0)
    deep_alpha = (2 * depth) ** 0.25
    scale = hd ** -0.5 * LOG2E
    tm = TOKEN_TILE

    wi = w_in[0]
    o3, o4 = 3 * fox_w, 3 * fox_w + h_fox
    w_main = jnp.concatenate([wi[:, :o3], wi[:, o4:]], axis=1).astype(BF16)
    w_fg = jnp.pad(wi[:, o3:o4], ((0, 0), (0, LANES - h_fox))).astype(BF16)
    b_fg = jnp.pad(b_fgate[0], (0, LANES - h_fox)).reshape(1, LANES)
    lam_rows = jnp.stack([lambda_q1[0], lambda_k1[0], lambda_q2[0], lambda_k2[0]])
    g_sub = g_subln[0].reshape(1, dd)
    slopes = 2.0 ** (-8.0 * jnp.arange(1, h_diff + 1, dtype=F32) / h_diff)

    xp2 = x_prompt.reshape(n_prompt, d_model)
    qf, kf, vf, qd, kd, vd, lf = _proj(xp2, w_main, w_fg, b_fg, tm=tm, width=fox_w, scale=scale)
    logf_p = lf[:, :h_fox]
    lf_t = jnp.swapaxes(logf_p.reshape(batch, seq, h_fox), 1, 2)
    t = ATTN_TILE
    c_rows = _cumsum(lf_t).reshape(batch * h_fox, seq // t, 1, t)
    b3 = lambda a: a.reshape(batch, seq, a.shape[-1])
    o_f = _fox_attn(b3(qf), b3(kf), b3(vf), c_rows, batch=batch, seq=seq, heads=h_fox, hd=hd, t=t)
    o_d = _diff_attn(slopes, b3(qd), b3(kd), b3(vd), lam_rows, g_sub,
                     batch=batch, seq=seq, heads=h_diff, hd=hd, t=t, lam_init=lam_init)

    xs2 = x_sample.reshape(dec_batch, d_model)
    qfs, kfs, vfs, qds, kds, vds, lfs = _proj(xs2, w_main, w_fg, b_fg, tm=dec_batch, width=fox_w, scale=scale)
    logf_s = lfs[:, :h_fox]
    lf_pages = jnp.swapaxes(cache_fox_logf[0], 1, 2)
    dec, tot = _decay_pre(lf_pages, group=math.gcd(n_pool, 64))
    o_fs = _fox_decode(page_table, qfs.reshape(dec_batch, h_fox, hd), kfs.reshape(dec_batch, h_fox, hd),
                       vfs.reshape(dec_batch, h_fox, hd), logf_s.reshape(dec_batch, h_fox, 1),
                       cache_fox_k.reshape(n_pool * page * h_fox, hd),
                       cache_fox_v.reshape(n_pool * page * h_fox, hd), dec, tot,
                       heads=h_fox, hd=hd, page=page)
    slope_rows = jnp.broadcast_to(jnp.repeat(slopes, 2)[:, None], (2 * h_diff, page * h_diff))
    o_ds = _diff_decode(page_table, qds.reshape(dec_batch, 2 * h_diff, hd), kds.reshape(dec_batch, 2 * h_diff, hd),
                        jnp.repeat(vds.reshape(dec_batch, h_diff, dd), 2, axis=1), slope_rows, lam_rows, g_sub,
                        cache_diff_k.reshape(n_pool, page, h_diff, dd),
                        cache_diff_v.reshape(n_pool, page, h_diff, dd),
                        heads=h_diff, hd=hd, page=page, lam_init=lam_init)

    pad_rows = lambda a: jnp.pad(a, ((0, tm - a.shape[0]), (0, 0)))
    prompt_in = (xp2, o_f.reshape(n_prompt, fox_w), o_d.reshape(n_prompt, diff_w),
                 p_prompt[0].reshape(n_prompt, -1))
    sample_in = (pad_rows(xs2), pad_rows(o_fs.reshape(dec_batch, fox_w).astype(BF16)),
                 pad_rows(o_ds.reshape(dec_batch, diff_w).astype(BF16)),
                 pad_rows(p_sample[0].reshape(dec_batch, -1)))
    w_r = jnp.pad(w_router[0], ((0, 0), (0, LANES - n_experts))).astype(BF16)
    b_r = jnp.pad(b_router[0], (0, LANES - n_experts), constant_values=NEG_INF).reshape(1, LANES)
    weights = (w_o[0].astype(BF16), w_pg[0].astype(BF16), w_pe[0].astype(BF16), w_r, b_r,
               ln1_g[0].reshape(1, d_model), ln1_b[0].reshape(1, d_model))
    n_tiles = n_prompt // tm + 1
    h3, base, eidx, wts, cnt = _post(prompt_in, sample_in, weights, n_tiles=n_tiles, tm=tm,
                                     alpha=deep_alpha, n_experts=n_experts)

    rows = ROW_TILE
    cnt = cnt[:, 0, :].astype(I32)
    totals = jnp.sum(cnt, axis=0)
    padded = ((totals + rows - 1) // rows) * rows
    group_end = jnp.cumsum(padded)
    group_off = group_end - padded
    start = group_off[None, :] + jnp.cumsum(cnt, axis=0) - cnt
    n_tot = n_tiles * tm
    n_blocks = (n_tot * TOP_K) // rows + n_experts
    n_used = (group_end[n_experts - 1] // rows).astype(I32).reshape(1)
    blk = jnp.minimum(jnp.arange(n_blocks, dtype=I32), n_used[0] - 1)
    block_expert = jnp.sum((blk * rows)[:, None] >= group_end[None, :n_experts], axis=1).astype(I32)
    tail = (group_off + totals)[:n_experts].astype(I32)
    dest = _rank(eidx, start.astype(F32).reshape(n_tiles, 1, LANES), tm=tm)
    dest_flat = dest[:, :TOP_K].reshape(n_tot * TOP_K)

    nw = d_model // LANES
    pad = (padded - totals)[:n_experts].astype(I32)
    xs = _dispatch(tail, pad, n_used, dest_flat, h3, tm=tm, nw=nw, rows=rows, n_blocks=n_blocks)
    b1 = b_mlp1[0]
    b1_grouped = b1.reshape(n_experts, -1, LANES, 2).swapaxes(2, 3).reshape(n_experts, 1, b1.shape[-1])
    act = _mlp1(block_expert, n_used, xs, w_mlp1[0], b1_grouped, rows=rows, nw=nw, tn=2 * 1024)
    y3 = _mlp2(block_expert, n_used, act, w_mlp2[0], b_mlp2[0][:, None, :], rows=rows)
    y_p, y_s = _combine(dest_flat, wts, base, ln2_g[0].reshape(1, d_model), ln2_b[0].reshape(1, d_model), y3,
                        tm=tm, n_prompt=n_prompt)

    p5 = lambda a, h, w: a.reshape(1, batch, seq, h, w)
    s5 = lambda a, h, w: a.reshape(1, dec_batch, 1, h, w)
    return (y_p.reshape(batch, seq, d_model), y_s[:dec_batch].reshape(dec_batch, 1, d_model),
            p5(kf, h_fox, hd), p5(vf, h_fox, hd), logf_p.reshape(1, batch, seq, h_fox),
            p5(kd, h_diff, dd), p5(vd, h_diff, dd),
            s5(kfs, h_fox, hd), s5(vfs, h_fox, hd), logf_s.reshape(1, dec_batch, 1, h_fox),
            s5(kds, h_diff, dd), s5(vds, h_diff, dd))
```
